```python
import jax, jax.numpy as jnp
from jax import lax
import numpy as np

D_MODEL = 2048
BATCH = 1
SEQ = 8192
DEPTH = 1
DEC_BATCH = 4
DEC_SEQ = 2048
PAST_LEN = 128

HEAD_DIM = 128
N_HEADS_A = 8
N_KV_A = 2
N_HEADS_B = 8
D_A = N_HEADS_A * HEAD_DIM
D_KV_A = N_KV_A * HEAD_DIM
D_B = N_HEADS_B * HEAD_DIM
D_MIX = D_A + D_B
D_IN = D_A + 2 * D_KV_A + D_A + 4 * D_B
N_META = 16
GRID_W = 64
WIN_R_MAX = 8
WIN_C = 16
Q_BLOCK = 128
ROPE_THETA = 10000.0
EPS = 1e-6

kernel_name = "hymba_gqa_natten_encoder"


def rms_norm(x, w):
    xf = x.astype(jnp.float32)
    y = xf * lax.rsqrt(jnp.mean(xf * xf, axis=-1, keepdims=True) + EPS)
    return (y * w.astype(jnp.float32)).astype(x.dtype)


def split_points():
    sizes = [D_A, D_KV_A, D_KV_A, D_A, D_B, D_B, D_B, D_B]
    pts, acc = [], 0
    for s in sizes[:-1]:
        acc += s
        pts.append(acc)
    return pts


def axial_rope_tables(n_tok):
    t = jnp.arange(n_tok, dtype=jnp.int32)
    row = jnp.concatenate([jnp.full((N_META,), -1, jnp.int32), t // GRID_W]).astype(jnp.float32)
    col = jnp.concatenate([jnp.arange(N_META, dtype=jnp.int32), t % GRID_W]).astype(jnp.float32)
    half = HEAD_DIM // 2
    inv_freq = ROPE_THETA ** (-jnp.arange(0, half, 2, dtype=jnp.float32) / half)
    ang_r = row[:, None] * inv_freq[None, :]
    ang_c = col[:, None] * inv_freq[None, :]
    return (jnp.cos(ang_r), jnp.sin(ang_r), jnp.cos(ang_c), jnp.sin(ang_c))


def _rotate(x, cos, sin):
    x1, x2 = jnp.split(x, 2, axis=-1)
    c = cos[None, :, None, :]
    s = sin[None, :, None, :]
    return jnp.concatenate([x1 * c - x2 * s, x1 * s + x2 * c], axis=-1)


def apply_axial_rope(x, tabs):
    cr, sr, cc, sc = tabs
    xr, xc = jnp.split(x.astype(jnp.float32), 2, axis=-1)
    return jnp.concatenate([_rotate(xr, cr, sr), _rotate(xc, cc, sc)], axis=-1).astype(x.dtype)


def global_gqa(q, k, v):
    B, N = q.shape[0], q.shape[1]
    G = N_HEADS_A // N_KV_A
    scale = HEAD_DIM ** -0.5

    def attend(qb):
        L = qb.shape[1]
        qg = qb.reshape(B, L, N_KV_A, G, HEAD_DIM)
        s = jnp.einsum('blkgd,bskd->bkgls', qg, k).astype(jnp.float32) * scale
        p = jax.nn.softmax(s, axis=-1).astype(v.dtype)
        o = jnp.einsum('bkgls,bskd->blkgd', p, v)
        return o.reshape(B, L, D_A)

    o_meta = attend(q[:, :N_META])
    S = N - N_META
    nb = S // Q_BLOCK
    q_blocks = q[:, N_META:].reshape(B, nb, Q_BLOCK, N_HEADS_A, HEAD_DIM).transpose(1, 0, 2, 3, 4)
    o_real = lax.map(attend, q_blocks)
    o_real = o_real.transpose(1, 0, 2, 3).reshape(B, S, D_A)
    return jnp.concatenate([o_meta, o_real], axis=1)


def neighbourhood_tables(n_tok):
    rows = n_tok // GRID_W
    kr = min(WIN_R_MAX, rows)
    t = jnp.arange(n_tok, dtype=jnp.int32)
    r = t // GRID_W
    c = t % GRID_W
    r0 = jnp.clip(r - kr // 2, 0, rows - kr)
    c0 = jnp.clip(c - WIN_C // 2, 0, GRID_W - WIN_C)
    kr_idx = r0[:, None] + jnp.arange(kr, dtype=jnp.int32)[None, :]
    kc_idx = c0[:, None] + jnp.arange(WIN_C, dtype=jnp.int32)[None, :]
    key_idx = (kr_idx[:, :, None] * GRID_W + kc_idx[:, None, :]).reshape(n_tok, kr * WIN_C)
    off_r = kr_idx - r[:, None] + (WIN_R_MAX - 1)
    off_c = kc_idx - c[:, None] + (WIN_C - 1)
    bias_idx = (off_r[:, :, None] * (2 * WIN_C - 1) + off_c[:, None, :]).reshape(n_tok, kr * WIN_C)
    return key_idx, bias_idx


def neighbourhood_attention(q, k, v, rpb):
    B, N, H, Dh = q.shape
    S = N - N_META
    key_idx, bias_idx = neighbourhood_tables(S)
    W = key_idx.shape[1]
    rpb_flat = rpb.reshape(H, -1).astype(jnp.float32)
    scale = Dh ** -0.5
    qm, km, vm = q[:, :N_META], k[:, :N_META], v[:, :N_META]
    kx, vx = k[:, N_META:], v[:, N_META:]
    sm = jnp.einsum('bqhd,bkhd->bhqk', qm, km).astype(jnp.float32) * scale
    o_meta = jnp.einsum('bhqk,bkhd->bqhd', jax.nn.softmax(sm, axis=-1).astype(v.dtype), vm)

    def attend(args):
        qb, kidx, bidx = args
        kg = kx[:, kidx]
        vg = vx[:, kidx]
        s_win = jnp.einsum('bqhd,bqwhd->bhqw', qb, kg).astype(jnp.float32) * scale + rpb_flat[:, bidx][None]
        s_meta = jnp.einsum('bqhd,bkhd->bhqk', qb, km).astype(jnp.float32) * scale
        p = jax.nn.softmax(jnp.concatenate([s_meta, s_win], axis=-1), axis=-1).astype(v.dtype)
        return (jnp.einsum('bhqk,bkhd->bqhd', p[..., :N_META], vm)
                + jnp.einsum('bhqw,bqwhd->bqhd', p[..., N_META:], vg))

    nb = S // Q_BLOCK
    q_blocks = q[:, N_META:].reshape(B, nb, Q_BLOCK, H, Dh).transpose(1, 0, 2, 3, 4)
    o_real = lax.map(attend, (q_blocks, key_idx.reshape(nb, Q_BLOCK, W), bias_idx.reshape(nb, Q_BLOCK, W)))
    o_real = o_real.transpose(1, 0, 2, 3, 4).reshape(B, S, H * Dh)
    return jnp.concatenate([o_meta.reshape(B, N_META, H * Dh), o_real], axis=1)


def encoder_layer(h, norm_w, w_in, q_norm_a, k_norm_a, q_norm_b, k_norm_b, rpb, w_out, rope_tabs):
    B, N, _ = h.shape
    u = rms_norm(h, norm_w) @ w_in.astype(h.dtype)
    qa, ka, va, za, qb, kb, vb, zb = jnp.split(u, split_points(), axis=-1)
    qa = apply_axial_rope(rms_norm(qa.reshape(B, N, N_HEADS_A, HEAD_DIM), q_norm_a), rope_tabs)
    ka = apply_axial_rope(rms_norm(ka.reshape(B, N, N_KV_A, HEAD_DIM), k_norm_a), rope_tabs)
    va = va.reshape(B, N, N_KV_A, HEAD_DIM)
    ya = global_gqa(qa, ka, va) * jax.nn.silu(za)
    qb = rms_norm(qb.reshape(B, N, N_HEADS_B, HEAD_DIM), q_norm_b)
    kb = rms_norm(kb.reshape(B, N, N_HEADS_B, HEAD_DIM), k_norm_b)
    vb = vb.reshape(B, N, N_HEADS_B, HEAD_DIM)
    yb = neighbourhood_attention(qb, kb, vb, rpb) * jax.nn.silu(zb)
    y = jnp.concatenate([ya, yb], axis=-1) @ w_out.astype(h.dtype)
    return h + y


def setup_inputs(seed: int = 0) -> dict:
    key = jax.random.key(seed)
    ks = jax.random.split(key, 12)
    f32 = jnp.float32
    return {
        "x_prompt": jax.random.normal(ks[0], (BATCH, SEQ, D_MODEL), f32),
        "x_sample": jax.random.normal(ks[1], (DEC_BATCH, DEC_SEQ, D_MODEL), f32),
        "meta_tokens": jax.random.normal(ks[2], (N_META, D_MODEL), f32),
        "norm_w": 1.0 + 0.02 * jax.random.normal(ks[3], (DEPTH, D_MODEL), f32),
        "w_in": jax.random.normal(ks[4], (DEPTH, D_MODEL, D_IN), f32) * D_MODEL ** -0.5,
        "q_norm_a": 1.0 + 0.02 * jax.random.normal(ks[5], (DEPTH, HEAD_DIM), f32),
        "k_norm_a": 1.0 + 0.02 * jax.random.normal(ks[6], (DEPTH, HEAD_DIM), f32),
        "q_norm_b": 1.0 + 0.02 * jax.random.normal(ks[7], (DEPTH, HEAD_DIM), f32),
        "k_norm_b": 1.0 + 0.02 * jax.random.normal(ks[8], (DEPTH, HEAD_DIM), f32),
        "rpb": 0.1 * jax.random.normal(ks[9], (DEPTH, N_HEADS_B, 2 * WIN_R_MAX - 1, 2 * WIN_C - 1), f32),
        "w_out": jax.random.normal(ks[10], (DEPTH, D_MIX, D_MODEL), f32) * D_MIX ** -0.5,
    }


def reference(x_prompt, x_sample, meta_tokens, norm_w, w_in, q_norm_a, k_norm_a, q_norm_b, k_norm_b, rpb, w_out):
    def encode(x):
        B, S, _ = x.shape
        meta = jnp.broadcast_to(meta_tokens.astype(x.dtype)[None], (B, N_META, D_MODEL))
        h = jnp.concatenate([meta, x], axis=1)
        tabs = axial_rope_tables(S)
        for l in range(DEPTH):
            h = encoder_layer(h, norm_w[l], w_in[l], q_norm_a[l], k_norm_a[l],
                              q_norm_b[l], k_norm_b[l], rpb[l], w_out[l], tabs)
        return h[:, N_META:]

    y_prompt = encode(x_prompt)
    y_sample = encode(x_sample)
    return (y_prompt, y_sample)
```

```python
import functools
import math

import jax
import jax.numpy as jnp
import numpy as np
from jax import lax
from jax.experimental import pallas as pl
from jax.experimental.pallas import tpu as pltpu

D_MODEL = 2048
HEAD_DIM = 128
N_HEADS_A = 8
N_KV_A = 2
GROUP_A = N_HEADS_A // N_KV_A
N_HEADS_B = 8
D_A = N_HEADS_A * HEAD_DIM
D_KV_A = N_KV_A * HEAD_DIM
D_B = N_HEADS_B * HEAD_DIM
D_IN = D_A + 2 * D_KV_A + D_A + 4 * D_B
N_META = 16
GRID_W = 64
WIN_R = 8
WIN_C = 16
ROPE_THETA = 10000.0
EPS = 1e-6

LOG2E = math.log2(math.e)
Q_SCALE = HEAD_DIM ** -0.5 * LOG2E
MASK_VALUE = -1e30

LANES = 128
META_PAD = LANES
VMEM_LIMIT = 56 * 1024 * 1024

TN = 512
N_COL_TILES = D_IN // TN
TILES_PER_QUERY_ROWS = 2
NB_Q = TILES_PER_QUERY_ROWS * GRID_W
NB_KROWS = WIN_R + TILES_PER_QUERY_ROWS - 1
NB_K = NB_KROWS * GRID_W
NB_CLASSES = 5


def _head_norm(a, gain):
    ms = jnp.mean(a * a, axis=-1, keepdims=True)
    return a * lax.rsqrt(ms + EPS) * gain


def _rope(y, cos, sin_a, sin_b):
    return y * cos + pltpu.roll(y, 96, 1) * sin_a + pltpu.roll(y, 32, 1) * sin_b


def _inproj_kernel(x_ref, nw_ref, w_ref, gqa_ref, gka_ref, gqb_ref, gkb_ref,
                   cos_ref, sa_ref, sb_ref,
                   qa_ref, ka_ref, va_ref, za_ref, qb_ref, kb_ref, vb_ref, zb_ref,
                   xn_ref):
    j = pl.program_id(1)

    @pl.when(j == 0)
    def _():
        x = x_ref[...]
        ms = jnp.mean(x * x, axis=-1, keepdims=True)
        xn_ref[...] = (x * lax.rsqrt(ms + EPS) * nw_ref[...]).astype(jnp.bfloat16)

    acc = jnp.dot(xn_ref[...], w_ref[...], preferred_element_type=jnp.float32)
    heads = [acc[:, h * HEAD_DIM:(h + 1) * HEAD_DIM] for h in range(TN // HEAD_DIM)]

    def normed_roped(a, gain_ref):
        return _rope(_head_norm(a, gain_ref[...]), cos_ref[...], sa_ref[...], sb_ref[...])

    def store_heads(ref, vals):
        for h, v in enumerate(vals):
            ref[:, h * HEAD_DIM:(h + 1) * HEAD_DIM] = v.astype(ref.dtype)

    @pl.when(j < 2)
    def _():
        store_heads(qa_ref, [normed_roped(a, gqa_ref) * Q_SCALE for a in heads])

    @pl.when(j == 2)
    def _():
        store_heads(ka_ref, [normed_roped(a, gka_ref) for a in heads[:2]])
        store_heads(va_ref, heads[2:])

    @pl.when((j == 3) | (j == 4))
    def _():
        za_ref[...] = acc

    @pl.when((j == 5) | (j == 6))
    def _():
        store_heads(qb_ref, [_head_norm(a, gqb_ref[...]) * Q_SCALE for a in heads])

    @pl.when((j == 7) | (j == 8))
    def _():
        store_heads(kb_ref, [_head_norm(a, gkb_ref[...]) for a in heads])

    @pl.when((j == 9) | (j == 10))
    def _():
        vb_ref[...] = acc.astype(vb_ref.dtype)

    @pl.when(j >= 11)
    def _():
        zb_ref[...] = acc


def _in_projection(x2d, seq, tm, norm_w, w_in_bf16, gains, tabs):
    m = x2d.shape[0]
    n_tab = seq // tm
    cos, sin_a, sin_b = tabs

    def col(lo):
        return lambda i, j: (i, jnp.clip(j - lo, 0, 1))

    row_only = lambda i, j: (i, 0)
    const = lambda i, j: (0, 0)
    tab_map = lambda i, j: (i % n_tab, 0)
    bf, f32 = jnp.bfloat16, jnp.float32
    out_shape = [
        jax.ShapeDtypeStruct((m, D_A), bf), jax.ShapeDtypeStruct((m, D_KV_A), bf),
        jax.ShapeDtypeStruct((m, D_KV_A), bf), jax.ShapeDtypeStruct((m, D_A), f32),
        jax.ShapeDtypeStruct((m, D_B), bf), jax.ShapeDtypeStruct((m, D_B), bf),
        jax.ShapeDtypeStruct((m, D_B), bf), jax.ShapeDtypeStruct((m, D_B), f32),
    ]
    out_specs = [
        pl.BlockSpec((tm, TN), col(0)), pl.BlockSpec((tm, D_KV_A), row_only),
        pl.BlockSpec((tm, D_KV_A), row_only), pl.BlockSpec((tm, TN), col(3)),
        pl.BlockSpec((tm, TN), col(5)), pl.BlockSpec((tm, TN), col(7)),
        pl.BlockSpec((tm, TN), col(9)), pl.BlockSpec((tm, TN), col(11)),
    ]
    gain_spec = pl.BlockSpec((1, HEAD_DIM), const)
    tab_spec = pl.BlockSpec((tm, HEAD_DIM), tab_map)
    return pl.pallas_call(
        _inproj_kernel,
        grid=(m // tm, N_COL_TILES),
        in_specs=[
            pl.BlockSpec((tm, D_MODEL), row_only),
            pl.BlockSpec((1, D_MODEL), const),
            pl.BlockSpec((D_MODEL, TN), lambda i, j: (0, j)),
            gain_spec, gain_spec, gain_spec, gain_spec,
            tab_spec, tab_spec, tab_spec,
        ],
        out_specs=out_specs,
        out_shape=out_shape,
        scratch_shapes=[pltpu.VMEM((tm, D_MODEL), bf)],
        name="in_projection",
        compiler_params=pltpu.CompilerParams(
            dimension_semantics=("arbitrary", "arbitrary"),
            vmem_limit_bytes=VMEM_LIMIT),
    )(x2d, norm_w, w_in_bf16, *gains, cos, sin_a, sin_b)


def _nt_dot(a, b):
    return lax.dot_general(a, b, (((1,), (1,)), ((), ())), preferred_element_type=jnp.float32)


def _silu(z):
    return z * (1.0 / (1.0 + jnp.exp(-z)))


def _gqa_kernel(q_ref, k_ref, v_ref, km_ref, vm_ref, z_ref, o_ref, *, tkv):
    tq = q_ref.shape[0]
    n_chunks = k_ref.shape[0] // tkv
    lane = lax.broadcasted_iota(jnp.int32, (tq, META_PAD), 1)
    meta_bias = jnp.where(lane < N_META, 0.0, MASK_VALUE).astype(jnp.float32)
    km = km_ref[0]
    vm = vm_ref[0]
    qs = [q_ref[:, h * HEAD_DIM:(h + 1) * HEAD_DIM] for h in range(GROUP_A)]

    carry = []
    for q in qs:
        s = _nt_dot(q, km) + meta_bias
        m = jnp.max(s, axis=-1, keepdims=True)
        p = jnp.exp2(s - m)
        l = jnp.sum(p, axis=-1, keepdims=True)
        acc = jnp.dot(p.astype(jnp.bfloat16), vm, preferred_element_type=jnp.float32)
        carry += [m, l, acc]

    def body(c, carry):
        start = pl.multiple_of(c * tkv, tkv)
        k = k_ref[pl.ds(start, tkv), :]
        v = v_ref[pl.ds(start, tkv), :]
        out = []
        for h, q in enumerate(qs):
            m, l, acc = carry[3 * h:3 * h + 3]
            s = _nt_dot(q, k)
            m_new = jnp.maximum(m, jnp.max(s, axis=-1, keepdims=True))
            alpha = jnp.exp2(m - m_new)
            p = jnp.exp2(s - m_new)
            l = alpha * l + jnp.sum(p, axis=-1, keepdims=True)
            acc = alpha * acc + jnp.dot(p.astype(jnp.bfloat16), v,
                                        preferred_element_type=jnp.float32)
            out += [m_new, l, acc]
        return tuple(out)

    carry = lax.fori_loop(0, n_chunks, body, tuple(carry))
    for h in range(GROUP_A):
        l, acc = carry[3 * h + 1], carry[3 * h + 2]
        sl = slice(h * HEAD_DIM, (h + 1) * HEAD_DIM)
        o_ref[:, sl] = (acc * (1.0 / l) * _silu(z_ref[:, sl])).astype(o_ref.dtype)


def _global_attention(qa, ka, va, za, km, vm, batch, seq, tq, tkv):
    m = qa.shape[0]
    nq = seq // tq
    qmap = lambda b, g, i: (b * nq + i, g)
    kvmap = lambda b, g, i: (b, g)
    meta_map = lambda b, g, i: (g, 0, 0)
    width = GROUP_A * HEAD_DIM
    return pl.pallas_call(
        functools.partial(_gqa_kernel, tkv=tkv),
        grid=(batch, N_KV_A, nq),
        in_specs=[
            pl.BlockSpec((tq, width), qmap),
            pl.BlockSpec((seq, HEAD_DIM), kvmap),
            pl.BlockSpec((seq, HEAD_DIM), kvmap),
            pl.BlockSpec((1, META_PAD, HEAD_DIM), meta_map),
            pl.BlockSpec((1, META_PAD, HEAD_DIM), meta_map),
            pl.BlockSpec((tq, width), qmap),
        ],
        out_specs=pl.BlockSpec((tq, width), qmap),
        out_shape=jax.ShapeDtypeStruct((m, D_A), jnp.bfloat16),
        name="global_attention",
        compiler_params=pltpu.CompilerParams(
            dimension_semantics=("arbitrary", "arbitrary", "arbitrary"),
            vmem_limit_bytes=VMEM_LIMIT),
    )(qa, ka, va, km, vm, za)


def _nbr_kernel(q_ref, k_ref, v_ref, km_ref, vm_ref, bias_ref, z_ref, o_ref, *, n_tiles):
    tiles_per_step = q_ref.shape[0] // NB_Q
    n_rows = n_tiles * TILES_PER_QUERY_ROWS
    lane = lax.broadcasted_iota(jnp.int32, (NB_Q, META_PAD), 1)
    meta_bias = jnp.where(lane < N_META, 0.0, MASK_VALUE).astype(jnp.float32)
    km = km_ref[0]
    vm = vm_ref[0]
    step = pl.program_id(2)

    def body(i, _):
        t = step * tiles_per_step + i
        cls = jnp.where(t < 2, t, jnp.where(t >= n_tiles - 2, t - n_tiles + NB_CLASSES, 2))
        k0 = jnp.clip(t * TILES_PER_QUERY_ROWS - WIN_R // 2, 0, n_rows - NB_KROWS)
        kstart = pl.multiple_of(k0 * GRID_W, GRID_W)
        qrows = pl.ds(pl.multiple_of(i * NB_Q, NB_Q), NB_Q)
        q = q_ref[qrows, :]
        k = k_ref[pl.ds(kstart, NB_K), :]
        v = v_ref[pl.ds(kstart, NB_K), :]
        s_meta = _nt_dot(q, km) + meta_bias
        s_win = _nt_dot(q, k) + bias_ref[0, cls]
        m = jnp.maximum(jnp.max(s_meta, axis=-1, keepdims=True),
                        jnp.max(s_win, axis=-1, keepdims=True))
        p_meta = jnp.exp2(s_meta - m)
        p_win = jnp.exp2(s_win - m)
        l = jnp.sum(p_meta, axis=-1, keepdims=True) + jnp.sum(p_win, axis=-1, keepdims=True)
        acc = (jnp.dot(p_meta.astype(jnp.bfloat16), vm, preferred_element_type=jnp.float32)
               + jnp.dot(p_win.astype(jnp.bfloat16), v, preferred_element_type=jnp.float32))
        o_ref[qrows, :] = (acc * (1.0 / l) * _silu(z_ref[qrows, :])).astype(o_ref.dtype)
        return 0

    lax.fori_loop(0, tiles_per_step, body, 0)


def _neighbourhood_attention(qb, kb, vb, zb, km, vm, bias, batch, seq, tqb):
    m = qb.shape[0]
    nq = seq // tqb
    qmap = lambda b, h, i: (b * nq + i, h)
    kvmap = lambda b, h, i: (b, h)
    meta_map = lambda b, h, i: (h, 0, 0)
    return pl.pallas_call(
        functools.partial(_nbr_kernel, n_tiles=seq // NB_Q),
        grid=(batch, N_HEADS_B, nq),
        in_specs=[
            pl.BlockSpec((tqb, HEAD_DIM), qmap),
            pl.BlockSpec((seq, HEAD_DIM), kvmap),
            pl.BlockSpec((seq, HEAD_DIM), kvmap),
            pl.BlockSpec((1, META_PAD, HEAD_DIM), meta_map),
            pl.BlockSpec((1, META_PAD, HEAD_DIM), meta_map),
            pl.BlockSpec((1, NB_CLASSES, NB_Q, NB_K), lambda b, h, i: (h, 0, 0, 0)),
            pl.BlockSpec((tqb, HEAD_DIM), qmap),
        ],
        out_specs=pl.BlockSpec((tqb, HEAD_DIM), qmap),
        out_shape=jax.ShapeDtypeStruct((m, D_B), jnp.bfloat16),
        name="neighbourhood_attention",
        compiler_params=pltpu.CompilerParams(
            dimension_semantics=("arbitrary", "arbitrary", "arbitrary"),
            vmem_limit_bytes=VMEM_LIMIT),
    )(qb, kb, vb, km, vm, bias, zb)


def _outproj_kernel(x_ref, ya_ref, yb_ref, w_ref, o_ref):
    y = jnp.dot(ya_ref[...], w_ref[:D_A, :], preferred_element_type=jnp.float32)
    y = y + jnp.dot(yb_ref[...], w_ref[D_A:, :], preferred_element_type=jnp.float32)
    o_ref[...] = x_ref[...] + y


def _out_projection(x2d, ya, yb, w_out_bf16, tm):
    m = x2d.shape[0]
    row = lambda i: (i, 0)
    return pl.pallas_call(
        _outproj_kernel,
        grid=(m // tm,),
        in_specs=[
            pl.BlockSpec((tm, D_MODEL), row),
            pl.BlockSpec((tm, D_A), row),
            pl.BlockSpec((tm, D_B), row),
            pl.BlockSpec((D_A + D_B, D_MODEL), lambda i: (0, 0)),
        ],
        out_specs=pl.BlockSpec((tm, D_MODEL), row),
        out_shape=jax.ShapeDtypeStruct((m, D_MODEL), jnp.float32),
        name="out_projection",
        compiler_params=pltpu.CompilerParams(
            dimension_semantics=("arbitrary",),
            vmem_limit_bytes=VMEM_LIMIT),
    )(x2d, ya, yb, w_out_bf16)


def _rope_tables(row, col):
    half = HEAD_DIM // 2
    inv_freq = ROPE_THETA ** (-jnp.arange(0, half, 2, dtype=jnp.float32) / half)
    ang_r = row.astype(jnp.float32)[:, None] * inv_freq[None, :]
    ang_c = col.astype(jnp.float32)[:, None] * inv_freq[None, :]
    cr, sr, cc, sc = jnp.cos(ang_r), jnp.sin(ang_r), jnp.cos(ang_c), jnp.sin(ang_c)
    zero = jnp.zeros_like(sr)
    cos = jnp.concatenate([cr, cr, cc, cc], axis=-1)
    sin_a = jnp.concatenate([-sr, zero, -sc, zero], axis=-1)
    sin_b = jnp.concatenate([zero, sr, zero, sc], axis=-1)
    return cos, sin_a, sin_b


def _nbr_bias_tables(rpb):
    rows = 32
    tile_r = np.array([0, 2, 8, rows - 4, rows - 2])[:, None, None, None, None]
    dr = np.arange(TILES_PER_QUERY_ROWS)[None, :, None, None, None]
    c = np.arange(GRID_W)[None, None, :, None, None]
    jj = np.arange(NB_KROWS)[None, None, None, :, None]
    kc = np.arange(GRID_W)[None, None, None, None, :]
    k0 = np.clip(tile_r - WIN_R // 2, 0, rows - NB_KROWS)
    r = tile_r + dr
    kr = k0 + jj
    r0 = np.clip(r - WIN_R // 2, 0, rows - WIN_R)
    c0 = np.clip(c - WIN_C // 2, 0, GRID_W - WIN_C)
    valid = (kr >= r0) & (kr < r0 + WIN_R) & (kc >= c0) & (kc < c0 + WIN_C)
    off_r = np.clip(kr - r + (WIN_R - 1), 0, 2 * WIN_R - 2)
    off_c = np.clip(kc - c + (WIN_C - 1), 0, 2 * WIN_C - 2)
    idx = (off_r * (2 * WIN_C - 1) + off_c).reshape(NB_CLASSES, NB_Q, NB_K)
    valid = np.broadcast_to(valid, (NB_CLASSES, TILES_PER_QUERY_ROWS, GRID_W, NB_KROWS, GRID_W))
    valid = valid.reshape(NB_CLASSES, NB_Q, NB_K)
    idx = np.broadcast_to(idx, (NB_CLASSES, NB_Q, NB_K))
    rpb_flat = rpb.reshape(N_HEADS_B, -1).astype(jnp.float32) * LOG2E
    gathered = rpb_flat[:, idx]
    return jnp.where(valid[None], gathered, MASK_VALUE)


def _pad_meta(t, n_heads):
    t = t.reshape(N_META, n_heads, HEAD_DIM).transpose(1, 0, 2)
    return jnp.pad(t, ((0, 0), (0, META_PAD - N_META), (0, 0)))


def kernel(x_prompt, x_sample, meta_tokens, norm_w, w_in, q_norm_a, k_norm_a, q_norm_b, k_norm_b, rpb, w_out):
    w_in_bf = w_in[0].astype(jnp.bfloat16)
    w_out_bf = w_out[0].astype(jnp.bfloat16)
    nw = norm_w[0].reshape(1, D_MODEL)
    gains = [g[0].reshape(1, HEAD_DIM) for g in (q_norm_a, k_norm_a, q_norm_b, k_norm_b)]
    bias = _nbr_bias_tables(rpb[0])

    meta_pos = jnp.arange(N_META, dtype=jnp.int32)
    meta_tabs = _rope_tables(jnp.full((N_META,), -1, jnp.int32), meta_pos)
    meta_out = _in_projection(meta_tokens, N_META, N_META, nw, w_in_bf, gains, meta_tabs)
    _, ka_m, va_m, _, _, kb_m, vb_m, _ = meta_out
    km_a, vm_a = _pad_meta(ka_m, N_KV_A), _pad_meta(va_m, N_KV_A)
    km_b, vm_b = _pad_meta(kb_m, N_HEADS_B), _pad_meta(vb_m, N_HEADS_B)

    def encode(x):
        batch, seq, _ = x.shape
        x2d = x.reshape(batch * seq, D_MODEL)
        t = jnp.arange(seq, dtype=jnp.int32)
        tabs = _rope_tables(t // GRID_W, t % GRID_W)
        qa, ka, va, za, qb, kb, vb, zb = _in_projection(x2d, seq, 512, nw, w_in_bf, gains, tabs)
        ya = _global_attention(qa, ka, va, za, km_a, vm_a, batch, seq, tq=256, tkv=512)
        yb = _neighbourhood_attention(qb, kb, vb, zb, km_b, vm_b, bias, batch, seq, tqb=1024)
        y = _out_projection(x2d, ya, yb, w_out_bf, 512)
        return y.reshape(batch, seq, D_MODEL)

    return (encode(x_prompt), encode(x_sample))
```

```python
import functools
import math

import jax
import jax.numpy as jnp
import numpy as np
from jax import lax
from jax.experimental import pallas as pl
from jax.experimental.pallas import tpu as pltpu

D_MODEL = 2048
HEAD_DIM = 128
N_HEADS_A = 8
N_KV_A = 2
GROUP_A = N_HEADS_A // N_KV_A
N_HEADS_B = 8
D_A = N_HEADS_A * HEAD_DIM
D_KV_A = N_KV_A * HEAD_DIM
D_B = N_HEADS_B * HEAD_DIM
D_IN = D_A + 2 * D_KV_A + D_A + 4 * D_B
N_META = 16
GRID_W = 64
WIN_R = 8
WIN_C = 16
ROPE_THETA = 10000.0
EPS = 1e-6

LOG2E = math.log2(math.e)
Q_SCALE = HEAD_DIM ** -0.5 * LOG2E
MASK_VALUE = -1e30

LANES = 128
META_PAD = LANES
VMEM_LIMIT = 56 * 1024 * 1024

TN = 512
N_COL_TILES = D_IN // TN
TILES_PER_QUERY_ROWS = 2
NB_Q = TILES_PER_QUERY_ROWS * GRID_W
NB_KROWS = WIN_R + TILES_PER_QUERY_ROWS - 1
NB_K = NB_KROWS * GRID_W
NB_CLASSES = 5


def _head_norm(a, gain):
    ms = jnp.mean(a * a, axis=-1, keepdims=True)
    return a * lax.rsqrt(ms + EPS) * gain


def _rope(y, cos, sin_a, sin_b):
    return y * cos + pltpu.roll(y, 96, 1) * sin_a + pltpu.roll(y, 32, 1) * sin_b


def _inproj_kernel(x_ref, nw_ref, w_ref, gqa_ref, gka_ref, gqb_ref, gkb_ref,
                   cos_ref, sa_ref, sb_ref,
                   qa_ref, ka_ref, va_ref, za_ref, qb_ref, kb_ref, vb_ref, zb_ref,
                   xn_ref):
    j = pl.program_id(1)

    @pl.when(j == 0)
    def _():
        x = x_ref[...]
        ms = jnp.mean(x * x, axis=-1, keepdims=True)
        xn_ref[...] = (x * lax.rsqrt(ms + EPS) * nw_ref[...]).astype(jnp.bfloat16)

    acc = jnp.dot(xn_ref[...], w_ref[...], preferred_element_type=jnp.float32)
    heads = [acc[:, h * HEAD_DIM:(h + 1) * HEAD_DIM] for h in range(TN // HEAD_DIM)]

    def normed_roped(a, gain_ref):
        return _rope(_head_norm(a, gain_ref[...]), cos_ref[...], sa_ref[...], sb_ref[...])

    def store_heads(ref, vals):
        for h, v in enumerate(vals):
            ref[:, h * HEAD_DIM:(h + 1) * HEAD_DIM] = v.astype(ref.dtype)

    @pl.when(j < 2)
    def _():
        store_heads(qa_ref, [normed_roped(a, gqa_ref) * Q_SCALE for a in heads])

    @pl.when(j == 2)
    def _():
        store_heads(ka_ref, [normed_roped(a, gka_ref) for a in heads[:2]])
        store_heads(va_ref, heads[2:])

    @pl.when((j == 3) | (j == 4))
    def _():
        za_ref[...] = acc

    @pl.when((j == 5) | (j == 6))
    def _():
        store_heads(qb_ref, [_head_norm(a, gqb_ref[...]) * Q_SCALE for a in heads])

    @pl.when((j == 7) | (j == 8))
    def _():
        store_heads(kb_ref, [_head_norm(a, gkb_ref[...]) for a in heads])

    @pl.when((j == 9) | (j == 10))
    def _():
        vb_ref[...] = acc.astype(vb_ref.dtype)

    @pl.when(j >= 11)
    def _():
        zb_ref[...] = acc


def _in_projection(x2d, seq, tm, norm_w, w_in_bf16, gains, tabs):
    m = x2d.shape[0]
    n_tab = seq // tm
    cos, sin_a, sin_b = tabs

    def col(lo):
        return lambda i, j: (i, jnp.clip(j - lo, 0, 1))

    row_only = lambda i, j: (i, 0)
    const = lambda i, j: (0, 0)
    tab_map = lambda i, j: (i % n_tab, 0)
    bf, f32 = jnp.bfloat16, jnp.float32
    out_shape = [
        jax.ShapeDtypeStruct((m, D_A), bf), jax.ShapeDtypeStruct((m, D_KV_A), bf),
        jax.ShapeDtypeStruct((m, D_KV_A), bf), jax.ShapeDtypeStruct((m, D_A), f32),
        jax.ShapeDtypeStruct((m, D_B), bf), jax.ShapeDtypeStruct((m, D_B), bf),
        jax.ShapeDtypeStruct((m, D_B), bf), jax.ShapeDtypeStruct((m, D_B), f32),
    ]
    out_specs = [
        pl.BlockSpec((tm, TN), col(0)), pl.BlockSpec((tm, D_KV_A), row_only),
        pl.BlockSpec((tm, D_KV_A), row_only), pl.BlockSpec((tm, TN), col(3)),
        pl.BlockSpec((tm, TN), col(5)), pl.BlockSpec((tm, TN), col(7)),
        pl.BlockSpec((tm, TN), col(9)), pl.BlockSpec((tm, TN), col(11)),
    ]
    gain_spec = pl.BlockSpec((1, HEAD_DIM), const)
    tab_spec = pl.BlockSpec((tm, HEAD_DIM), tab_map)
    return pl.pallas_call(
        _inproj_kernel,
        grid=(m // tm, N_COL_TILES),
        in_specs=[
            pl.BlockSpec((tm, D_MODEL), row_only),
            pl.BlockSpec((1, D_MODEL), const),
            pl.BlockSpec((D_MODEL, TN), lambda i, j: (0, j)),
            gain_spec, gain_spec, gain_spec, gain_spec,
            tab_spec, tab_spec, tab_spec,
        ],
        out_specs=out_specs,
        out_shape=out_shape,
        scratch_shapes=[pltpu.VMEM((tm, D_MODEL), bf)],
        name="in_projection",
        compiler_params=pltpu.CompilerParams(
            dimension_semantics=("arbitrary", "arbitrary"),
            vmem_limit_bytes=VMEM_LIMIT),
    )(x2d, norm_w, w_in_bf16, *gains, cos, sin_a, sin_b)


def _nt_dot(a, b):
    return lax.dot_general(a, b, (((1,), (1,)), ((), ())), preferred_element_type=jnp.float32)


def _silu(z):
    return z * (1.0 / (1.0 + jnp.exp(-z)))


def _gqa_kernel(q_ref, k_ref, v_ref, km_ref, vm_ref, z_ref, o_ref, *, tkv):
    tq = q_ref.shape[0]
    n_chunks = k_ref.shape[0] // tkv
    lane = lax.broadcasted_iota(jnp.int32, (tq, META_PAD), 1)
    meta_bias = jnp.where(lane < N_META, 0.0, MASK_VALUE).astype(jnp.float32)
    km = km_ref[0]
    vm = vm_ref[0]
    qs = [q_ref[:, h * HEAD_DIM:(h + 1) * HEAD_DIM] for h in range(GROUP_A)]

    carry = []
    for q in qs:
        s = _nt_dot(q, km) + meta_bias
        m = jnp.max(s, axis=-1, keepdims=True)
        p = jnp.exp2(s - m)
        l = jnp.sum(p, axis=-1, keepdims=True)
        acc = jnp.dot(p.astype(jnp.bfloat16), vm, preferred_element_type=jnp.float32)
        carry += [m, l, acc]

    def body(c, carry):
        start = pl.multiple_of(c * tkv, tkv)
        k = k_ref[pl.ds(start, tkv), :]
        v = v_ref[pl.ds(start, tkv), :]
        out = []
        for h, q in enumerate(qs):
            m, l, acc = carry[3 * h:3 * h + 3]
            s = _nt_dot(q, k)
            m_new = jnp.maximum(m, jnp.max(s, axis=-1, keepdims=True))
            alpha = jnp.exp2(m - m_new)
            p = jnp.exp2(s - m_new)
            l = alpha * l + jnp.sum(p, axis=-1, keepdims=True)
            acc = alpha * acc + jnp.dot(p.astype(jnp.bfloat16), v,
                                        preferred_element_type=jnp.float32)
            out += [m_new, l, acc]
        return tuple(out)

    carry = lax.fori_loop(0, n_chunks, body, tuple(carry))
    for h in range(GROUP_A):
        l, acc = carry[3 * h + 1], carry[3 * h + 2]
        sl = slice(h * HEAD_DIM, (h + 1) * HEAD_DIM)
        o_ref[:, sl] = (acc * (1.0 / l) * _silu(z_ref[:, sl])).astype(o_ref.dtype)


def _global_attention(qa, ka, va, za, km, vm, batch, seq, tq, tkv):
    m = qa.shape[0]
    nq = seq // tq
    qmap = lambda b, g, i: (b * nq + i, g)
    kvmap = lambda b, g, i: (b, g)
    meta_map = lambda b, g, i: (g, 0, 0)
    width = GROUP_A * HEAD_DIM
    return pl.pallas_call(
        functools.partial(_gqa_kernel, tkv=tkv),
        grid=(batch, N_KV_A, nq),
        in_specs=[
            pl.BlockSpec((tq, width), qmap),
            pl.BlockSpec((seq, HEAD_DIM), kvmap),
            pl.BlockSpec((seq, HEAD_DIM), kvmap),
            pl.BlockSpec((1, META_PAD, HEAD_DIM), meta_map),
            pl.BlockSpec((1, META_PAD, HEAD_DIM), meta_map),
            pl.BlockSpec((tq, width), qmap),
        ],
        out_specs=pl.BlockSpec((tq, width), qmap),
        out_shape=jax.ShapeDtypeStruct((m, D_A), jnp.bfloat16),
        name="global_attention",
        compiler_params=pltpu.CompilerParams(
            dimension_semantics=("arbitrary", "arbitrary", "arbitrary"),
            vmem_limit_bytes=VMEM_LIMIT),
    )(qa, ka, va, km, vm, za)


def _nbr_kernel(q_ref, k_ref, v_ref, km_ref, vm_ref, bias_ref, z_ref, o_ref, *, n_tiles):
    tiles_per_step = q_ref.shape[0] // NB_Q
    n_rows = n_tiles * TILES_PER_QUERY_ROWS
    lane = lax.broadcasted_iota(jnp.int32, (NB_Q, META_PAD), 1)
    meta_bias = jnp.where(lane < N_META, 0.0, MASK_VALUE).astype(jnp.float32)
    km = km_ref[0]
    vm = vm_ref[0]
    step = pl.program_id(2)

    def body(i, _):
        t = step * tiles_per_step + i
        cls = jnp.where(t < 2, t, jnp.where(t >= n_tiles - 2, t - n_tiles + NB_CLASSES, 2))
        k0 = jnp.clip(t * TILES_PER_QUERY_ROWS - WIN_R // 2, 0, n_rows - NB_KROWS)
        kstart = pl.multiple_of(k0 * GRID_W, GRID_W)
        qrows = pl.ds(pl.multiple_of(i * NB_Q, NB_Q), NB_Q)
        q = q_ref[qrows, :]
        k = k_ref[pl.ds(kstart, NB_K), :]
        v = v_ref[pl.ds(kstart, NB_K), :]
        s_meta = _nt_dot(q, km) + meta_bias
        s_win = _nt_dot(q, k) + bias_ref[0, cls]
        m = jnp.maximum(jnp.max(s_meta, axis=-1, keepdims=True),
                        jnp.max(s_win, axis=-1, keepdims=True))
        p_meta = jnp.exp2(s_meta - m)
        p_win = jnp.exp2(s_win - m)
        l = jnp.sum(p_meta, axis=-1, keepdims=True) + jnp.sum(p_win, axis=-1, keepdims=True)
        acc = (jnp.dot(p_meta.astype(jnp.bfloat16), vm, preferred_element_type=jnp.float32)
               + jnp.dot(p_win.astype(jnp.bfloat16), v, preferred_element_type=jnp.float32))
        o_ref[qrows, :] = (acc * (1.0 / l) * _silu(z_ref[qrows, :])).astype(o_ref.dtype)
        return 0

    lax.fori_loop(0, tiles_per_step, body, 0)


def _neighbourhood_attention(qb, kb, vb, zb, km, vm, bias, batch, seq, tqb):
    m = qb.shape[0]
    nq = seq // tqb
    qmap = lambda b, h, i: (b * nq + i, h)
    kvmap = lambda b, h, i: (b, h)
    meta_map = lambda b, h, i: (h, 0, 0)
    return pl.pallas_call(
        functools.partial(_nbr_kernel, n_tiles=seq // NB_Q),
        grid=(batch, N_HEADS_B, nq),
        in_specs=[
            pl.BlockSpec((tqb, HEAD_DIM), qmap),
            pl.BlockSpec((seq, HEAD_DIM), kvmap),
            pl.BlockSpec((seq, HEAD_DIM), kvmap),
            pl.BlockSpec((1, META_PAD, HEAD_DIM), meta_map),
            pl.BlockSpec((1, META_PAD, HEAD_DIM), meta_map),
            pl.BlockSpec((1, NB_CLASSES, NB_Q, NB_K), lambda b, h, i: (h, 0, 0, 0)),
            pl.BlockSpec((tqb, HEAD_DIM), qmap),
        ],
        out_specs=pl.BlockSpec((tqb, HEAD_DIM), qmap),
        out_shape=jax.ShapeDtypeStruct((m, D_B), jnp.bfloat16),
        name="neighbourhood_attention",
        compiler_params=pltpu.CompilerParams(
            dimension_semantics=("arbitrary", "arbitrary", "arbitrary"),
            vmem_limit_bytes=VMEM_LIMIT),
    )(qb, kb, vb, km, vm, bias, zb)


def _outproj_kernel(x_ref, ya_ref, yb_ref, w_ref, o_ref):
    y = jnp.dot(ya_ref[...], w_ref[:D_A, :], preferred_element_type=jnp.float32)
    y = y + jnp.dot(yb_ref[...], w_ref[D_A:, :], preferred_element_type=jnp.float32)
    o_ref[...] = x_ref[...] + y


def _out_projection(x2d, ya, yb, w_out_bf16, tm):
    m = x2d.shape[0]
    row = lambda i: (i, 0)
    return pl.pallas_call(
        _outproj_kernel,
        grid=(m // tm,),
        in_specs=[
            pl.BlockSpec((tm, D_MODEL), row),
            pl.BlockSpec((tm, D_A), row),
            pl.BlockSpec((tm, D_B), row),
            pl.BlockSpec((D_A + D_B, D_MODEL), lambda i: (0, 0)),
        ],
        out_specs=pl.BlockSpec((tm, D_MODEL), row),
        out_shape=jax.ShapeDtypeStruct((m, D_MODEL), jnp.float32),
        name="out_projection",
        compiler_params=pltpu.CompilerParams(
            dimension_semantics=("arbitrary",),
            vmem_limit_bytes=VMEM_LIMIT),
    )(x2d, ya, yb, w_out_bf16)


def _rope_tables(row, col):
    half = HEAD_DIM // 2
    inv_freq = ROPE_THETA ** (-jnp.arange(0, half, 2, dtype=jnp.float32) / half)
    ang_r = row.astype(jnp.float32)[:, None] * inv_freq[None, :]
    ang_c = col.astype(jnp.float32)[:, None] * inv_freq[None, :]
    cr, sr, cc, sc = jnp.cos(ang_r), jnp.sin(ang_r), jnp.cos(ang_c), jnp.sin(ang_c)
    zero = jnp.zeros_like(sr)
    cos = jnp.concatenate([cr, cr, cc, cc], axis=-1)
    sin_a = jnp.concatenate([-sr, zero, -sc, zero], axis=-1)
    sin_b = jnp.concatenate([zero, sr, zero, sc], axis=-1)
    return cos, sin_a, sin_b


def _nbr_bias_tables(rpb):
    rows = 32
    tile_r = np.array([0, 2, 8, rows - 4, rows - 2])[:, None, None, None, None]
    dr = np.arange(TILES_PER_QUERY_ROWS)[None, :, None, None, None]
    c = np.arange(GRID_W)[None, None, :, None, None]
    jj = np.arange(NB_KROWS)[None, None, None, :, None]
    kc = np.arange(GRID_W)[None, None, None, None, :]
    k0 = np.clip(tile_r - WIN_R // 2, 0, rows - NB_KROWS)
    r = tile_r + dr
    kr = k0 + jj
    r0 = np.clip(r - WIN_R // 2, 0, rows - WIN_R)
    c0 = np.clip(c - WIN_C // 2, 0, GRID_W - WIN_C)
    valid = (kr >= r0) & (kr < r0 + WIN_R) & (kc >= c0) & (kc < c0 + WIN_C)
    valid = np.broadcast_to(valid, (NB_CLASSES, TILES_PER_QUERY_ROWS, GRID_W, NB_KROWS, GRID_W))
    valid = valid.reshape(NB_CLASSES, NB_Q, NB_K)
    off_r = np.clip(kr - r + (WIN_R - 1), 0, 2 * WIN_R - 2).reshape(-1)
    n_off_c = 2 * WIN_C - 1
    pos = np.arange(GRID_W)
    onehot = (pos[None, None, :] - pos[None, :, None] + (WIN_C - 1)
              == np.arange(n_off_c)[:, None, None]).astype(np.float32)
    toeplitz = jnp.einsum('hro,ock->hrck', rpb.astype(jnp.float32) * LOG2E, onehot,
                          precision=lax.Precision.HIGHEST)
    tab = jnp.take(toeplitz, off_r, axis=1)
    tab = tab.reshape(N_HEADS_B, NB_CLASSES, TILES_PER_QUERY_ROWS, NB_KROWS, GRID_W, GRID_W)
    tab = tab.transpose(0, 1, 2, 4, 3, 5).reshape(N_HEADS_B, NB_CLASSES, NB_Q, NB_K)
    return jnp.where(valid[None], tab, MASK_VALUE)


def _pad_meta(t, n_heads):
    t = t.reshape(N_META, n_heads, HEAD_DIM).transpose(1, 0, 2)
    return jnp.pad(t, ((0, 0), (0, META_PAD - N_META), (0, 0)))


def kernel(x_prompt, x_sample, meta_tokens, norm_w, w_in, q_norm_a, k_norm_a, q_norm_b, k_norm_b, rpb, w_out):
    w_in_bf = w_in[0].astype(jnp.bfloat16)
    w_out_bf = w_out[0].astype(jnp.bfloat16)
    nw = norm_w[0].reshape(1, D_MODEL)
    gains = [g[0].reshape(1, HEAD_DIM) for g in (q_norm_a, k_norm_a, q_norm_b, k_norm_b)]
    bias = _nbr_bias_tables(rpb[0])

    meta_pos = jnp.arange(N_META, dtype=jnp.int32)
    meta_tabs = _rope_tables(jnp.full((N_META,), -1, jnp.int32), meta_pos)
    meta_out = _in_projection(meta_tokens, N_META, N_META, nw, w_in_bf, gains, meta_tabs)
    _, ka_m, va_m, _, _, kb_m, vb_m, _ = meta_out
    km_a, vm_a = _pad_meta(ka_m, N_KV_A), _pad_meta(va_m, N_KV_A)
    km_b, vm_b = _pad_meta(kb_m, N_HEADS_B), _pad_meta(vb_m, N_HEADS_B)

    def encode(x):
        batch, seq, _ = x.shape
        x2d = x.reshape(batch * seq, D_MODEL)
        t = jnp.arange(seq, dtype=jnp.int32)
        tabs = _rope_tables(t // GRID_W, t % GRID_W)
        qa, ka, va, za, qb, kb, vb, zb = _in_projection(x2d, seq, 512, nw, w_in_bf, gains, tabs)
        ya = _global_attention(qa, ka, va, za, km_a, vm_a, batch, seq, tq=256, tkv=512)
        yb = _neighbourhood_attention(qb, kb, vb, zb, km_b, vm_b, bias, batch, seq, tqb=1024)
        y = _out_projection(x2d, ya, yb, w_out_bf, 512)
        return y.reshape(batch, seq, D_MODEL)

    return (encode(x_prompt), encode(x_sample))
```

```python
import functools
import math

import jax
import jax.numpy as jnp
import numpy as np
from jax import lax
from jax.experimental import pallas as pl
from jax.experimental.pallas import tpu as pltpu

D_MODEL = 2048
HEAD_DIM = 128
N_HEADS_A = 8
N_KV_A = 2
GROUP_A = N_HEADS_A // N_KV_A
N_HEADS_B = 8
D_A = N_HEADS_A * HEAD_DIM
D_KV_A = N_KV_A * HEAD_DIM
D_B = N_HEADS_B * HEAD_DIM
D_IN = D_A + 2 * D_KV_A + D_A + 4 * D_B
N_META = 16
GRID_W = 64
WIN_R = 8
WIN_C = 16
ROPE_THETA = 10000.0
EPS = 1e-6

LOG2E = math.log2(math.e)
Q_SCALE = HEAD_DIM ** -0.5 * LOG2E
MASK_VALUE = -1e30

LANES = 128
META_PAD = LANES
VMEM_LIMIT = 56 * 1024 * 1024

TN = 512
N_COL_TILES = D_IN // TN
TILES_PER_QUERY_ROWS = 2
NB_Q = TILES_PER_QUERY_ROWS * GRID_W
NB_KROWS = WIN_R + TILES_PER_QUERY_ROWS - 1
NB_K = NB_KROWS * GRID_W
NB_CLASSES = 5

ROW_TILE = 512
GQA_Q_TILE = 256
NB_ROWS_PER_STEP = 1024


def _head_norm(a, gain):
    ms = jnp.mean(a * a, axis=-1, keepdims=True)
    return a * lax.rsqrt(ms + EPS) * gain


def _rope(y, cos, sin_a, sin_b):
    return y * cos + pltpu.roll(y, 96, 1) * sin_a + pltpu.roll(y, 32, 1) * sin_b


def _inproj_kernel(x_ref, nw_ref, w_ref, gqa_ref, gka_ref, gqb_ref, gkb_ref,
                   cos_ref, sa_ref, sb_ref,
                   qat_ref, ka_ref, vat_ref, za_ref, qb_ref, kb_ref, vb_ref, zb_ref,
                   xn_ref):
    j = pl.program_id(1)

    @pl.when(j == 0)
    def _():
        x = x_ref[...]
        ms = jnp.mean(x * x, axis=-1, keepdims=True)
        xn_ref[...] = (x * lax.rsqrt(ms + EPS) * nw_ref[...]).astype(jnp.bfloat16)

    acc = jnp.dot(xn_ref[...], w_ref[...], preferred_element_type=jnp.float32)
    heads = [acc[:, h * HEAD_DIM:(h + 1) * HEAD_DIM] for h in range(TN // HEAD_DIM)]

    def normed_roped(a, gain_ref):
        return _rope(_head_norm(a, gain_ref[...]), cos_ref[...], sa_ref[...], sb_ref[...])

    def store_heads(ref, vals):
        for h, v in enumerate(vals):
            ref[:, h * HEAD_DIM:(h + 1) * HEAD_DIM] = v.astype(ref.dtype)

    @pl.when(j < 2)
    def _():
        for h, a in enumerate(heads):
            q = normed_roped(a, gqa_ref) * Q_SCALE
            qat_ref[h] = q.T.astype(qat_ref.dtype)

    @pl.when(j == 2)
    def _():
        store_heads(ka_ref, [normed_roped(a, gka_ref) for a in heads[:2]])
        for g, a in enumerate(heads[2:]):
            vat_ref[g, 0] = a.T.astype(vat_ref.dtype)

    @pl.when((j == 3) | (j == 4))
    def _():
        za_ref[...] = acc

    @pl.when((j == 5) | (j == 6))
    def _():
        store_heads(qb_ref, [_head_norm(a, gqb_ref[...]) * Q_SCALE for a in heads])

    @pl.when((j == 7) | (j == 8))
    def _():
        store_heads(kb_ref, [_head_norm(a, gkb_ref[...]) for a in heads])

    @pl.when((j == 9) | (j == 10))
    def _():
        vb_ref[...] = acc.astype(vb_ref.dtype)

    @pl.when(j >= 11)
    def _():
        zb_ref[...] = acc


def _in_projection(x2d, seq, tm, norm_w, w_in_bf16, gains, tabs):
    m = x2d.shape[0]
    n_tab = seq // tm
    cos, sin_a, sin_b = tabs

    def col(lo):
        return lambda i, j: (i, jnp.clip(j - lo, 0, 1))

    row_only = lambda i, j: (i, 0)
    const = lambda i, j: (0, 0)
    tab_map = lambda i, j: (i % n_tab, 0)
    bf, f32 = jnp.bfloat16, jnp.float32
    out_shape = [
        jax.ShapeDtypeStruct((N_HEADS_A, HEAD_DIM, m), bf), jax.ShapeDtypeStruct((m, D_KV_A), bf),
        jax.ShapeDtypeStruct((N_KV_A, m // tm, HEAD_DIM, tm), bf), jax.ShapeDtypeStruct((m, D_A), f32),
        jax.ShapeDtypeStruct((m, D_B), bf), jax.ShapeDtypeStruct((m, D_B), bf),
        jax.ShapeDtypeStruct((m, D_B), bf), jax.ShapeDtypeStruct((m, D_B), f32),
    ]
    out_specs = [
        pl.BlockSpec((TN // HEAD_DIM, HEAD_DIM, tm), lambda i, j: (jnp.clip(j, 0, 1), 0, i)),
        pl.BlockSpec((tm, D_KV_A), row_only),
        pl.BlockSpec((N_KV_A, 1, HEAD_DIM, tm), lambda i, j: (0, i, 0, 0)),
        pl.BlockSpec((tm, TN), col(3)),
        pl.BlockSpec((tm, TN), col(5)), pl.BlockSpec((tm, TN), col(7)),
        pl.BlockSpec((tm, TN), col(9)), pl.BlockSpec((tm, TN), col(11)),
    ]
    gain_spec = pl.BlockSpec((1, HEAD_DIM), const)
    tab_spec = pl.BlockSpec((tm, HEAD_DIM), tab_map)
    return pl.pallas_call(
        _inproj_kernel,
        grid=(m // tm, N_COL_TILES),
        in_specs=[
            pl.BlockSpec((tm, D_MODEL), row_only),
            pl.BlockSpec((1, D_MODEL), const),
            pl.BlockSpec((D_MODEL, TN), lambda i, j: (0, j)),
            gain_spec, gain_spec, gain_spec, gain_spec,
            tab_spec, tab_spec, tab_spec,
        ],
        out_specs=out_specs,
        out_shape=out_shape,
        scratch_shapes=[pltpu.VMEM((tm, D_MODEL), bf)],
        name="in_projection",
        compiler_params=pltpu.CompilerParams(
            dimension_semantics=("arbitrary", "arbitrary"),
            vmem_limit_bytes=VMEM_LIMIT),
    )(x2d, norm_w, w_in_bf16, *gains, cos, sin_a, sin_b)


def _nt_dot(a, b):
    return lax.dot_general(a, b, (((1,), (1,)), ((), ())), preferred_element_type=jnp.float32)


def _silu(z):
    return z * (1.0 / (1.0 + jnp.exp(-z)))


GQA_PAIRS = GROUP_A // 2


def _gqa_kernel(qt_ref, k_ref, vt_ref, km_ref, vmt_ref, z_ref, o_ref, acc_ref, s_ref):
    tq = qt_ref.shape[2]
    n_chunks, chunk = vt_ref.shape[1], vt_ref.shape[3]
    width = 2 * tq
    row = lax.broadcasted_iota(jnp.int32, (META_PAD, width), 0)
    meta_bias = jnp.where(row < N_META, 0.0, MASK_VALUE).astype(jnp.float32)
    km = km_ref[...]
    vmt = vmt_ref[0, 0]
    qts = [jnp.concatenate([qt_ref[2 * i], qt_ref[2 * i + 1]], axis=1) for i in range(GQA_PAIRS)]

    def dot(a, b):
        return jnp.dot(a, b, preferred_element_type=jnp.float32)

    carry = []
    for i, qt in enumerate(qts):
        s = dot(km, qt) + meta_bias
        m = jnp.max(s, axis=0, keepdims=True)
        p = jnp.exp2(s - m)
        carry += [m, jnp.sum(p, axis=0, keepdims=True)]
        acc_ref[i] = dot(vmt, p.astype(jnp.bfloat16))

    def softmax_pv(i, s, m, l, vt):
        m_new = jnp.maximum(m, jnp.max(s, axis=0, keepdims=True))
        alpha = jnp.exp2(m - m_new)
        p = jnp.exp2(s - m_new)
        l = alpha * l + jnp.sum(p, axis=0, keepdims=True)
        acc_ref[i] = alpha * acc_ref[i] + dot(vt, p.astype(jnp.bfloat16))
        return m_new, l

    def keys(c):
        return k_ref[pl.ds(pl.multiple_of(c * chunk, chunk), chunk), :]

    s_ref[0] = dot(keys(0), qts[0])

    def body(c2, carry):
        m0, l0, m1, l1 = carry
        c = 2 * c2
        for slot in range(2):
            cur = c + slot
            nxt = jnp.minimum(cur + 1, n_chunks - 1)
            k = keys(cur)
            vt = vt_ref[0, cur]
            s1 = dot(k, qts[1])
            m0, l0 = softmax_pv(0, s_ref[slot], m0, l0, vt)
            s_ref[1 - slot] = dot(keys(nxt), qts[0])
            m1, l1 = softmax_pv(1, s1, m1, l1, vt)
        return m0, l0, m1, l1

    carry = lax.fori_loop(0, n_chunks // 2, body, tuple(carry))
    for i in range(GQA_PAIRS):
        o_pair = acc_ref[i] * (1.0 / carry[2 * i + 1])
        for j in range(2):
            h = 2 * i + j
            sl = slice(h * HEAD_DIM, (h + 1) * HEAD_DIM)
            o = o_pair[:, j * tq:(j + 1) * tq].T
            o_ref[:, sl] = (o * _silu(z_ref[:, sl])).astype(o_ref.dtype)


def _global_attention(qat, ka, vat, za, km, vmt, batch, seq, tq):
    m = za.shape[0]
    nq = seq // tq
    chunk = vat.shape[3]
    n_chunks = seq // chunk
    qmap = lambda b, g, i: (b * nq + i, g)
    width = GROUP_A * HEAD_DIM
    return pl.pallas_call(
        _gqa_kernel,
        grid=(batch, N_KV_A, nq),
        in_specs=[
            pl.BlockSpec((GROUP_A, HEAD_DIM, tq), lambda b, g, i: (g, 0, b * nq + i)),
            pl.BlockSpec((seq, HEAD_DIM), lambda b, g, i: (b, g)),
            pl.BlockSpec((1, n_chunks, HEAD_DIM, chunk), lambda b, g, i: (g, b, 0, 0)),
            pl.BlockSpec((META_PAD, HEAD_DIM), lambda b, g, i: (0, g)),
            pl.BlockSpec((1, 1, HEAD_DIM, META_PAD), lambda b, g, i: (g, 0, 0, 0)),
            pl.BlockSpec((tq, width), qmap),
        ],
        out_specs=pl.BlockSpec((tq, width), qmap),
        out_shape=jax.ShapeDtypeStruct((m, D_A), jnp.bfloat16),
        scratch_shapes=[pltpu.VMEM((GQA_PAIRS, HEAD_DIM, 2 * tq), jnp.float32),
                        pltpu.VMEM((2, chunk, 2 * tq), jnp.float32)],
        name="global_attention",
        compiler_params=pltpu.CompilerParams(
            dimension_semantics=("arbitrary", "arbitrary", "arbitrary"),
            vmem_limit_bytes=VMEM_LIMIT),
    )(qat, ka, vat, km, vmt, za)


def _nbr_kernel(q_ref, k_ref, v_ref, km_ref, vm_ref, bias_ref, z_ref, o_ref, *, n_tiles):
    tiles_per_step = q_ref.shape[0] // NB_Q
    n_rows = n_tiles * TILES_PER_QUERY_ROWS
    lane = lax.broadcasted_iota(jnp.int32, (NB_Q, META_PAD), 1)
    meta_bias = jnp.where(lane < N_META, 0.0, MASK_VALUE).astype(jnp.float32)
    km = km_ref[...]
    vm = vm_ref[...]
    step = pl.program_id(2)

    def body(i, _):
        t = step * tiles_per_step + i
        cls = jnp.where(t < 2, t, jnp.where(t >= n_tiles - 2, t - n_tiles + NB_CLASSES, 2))
        k0 = jnp.clip(t * TILES_PER_QUERY_ROWS - WIN_R // 2, 0, n_rows - NB_KROWS)
        kstart = pl.multiple_of(k0 * GRID_W, GRID_W)
        qrows = pl.ds(pl.multiple_of(i * NB_Q, NB_Q), NB_Q)
        q = q_ref[qrows, :]
        k = k_ref[pl.ds(kstart, NB_K), :]
        v = v_ref[pl.ds(kstart, NB_K), :]
        s_meta = _nt_dot(q, km) + meta_bias
        s_win = _nt_dot(q, k) + bias_ref[0, cls]
        m = jnp.maximum(jnp.max(s_meta, axis=-1, keepdims=True),
                        jnp.max(s_win, axis=-1, keepdims=True))
        p_meta = jnp.exp2(s_meta - m)
        p_win = jnp.exp2(s_win - m)
        l = jnp.sum(p_meta, axis=-1, keepdims=True) + jnp.sum(p_win, axis=-1, keepdims=True)
        acc = (jnp.dot(p_meta.astype(jnp.bfloat16), vm, preferred_element_type=jnp.float32)
               + jnp.dot(p_win.astype(jnp.bfloat16), v, preferred_element_type=jnp.float32))
        o_ref[qrows, :] = (acc * (1.0 / l) * _silu(z_ref[qrows, :])).astype(o_ref.dtype)
        return 0

    lax.fori_loop(0, tiles_per_step, body, 0)


def _neighbourhood_attention(qb, kb, vb, zb, km, vm, bias, batch, seq, tqb):
    m = qb.shape[0]
    nq = seq // tqb
    qmap = lambda b, h, i: (b * nq + i, h)
    kvmap = lambda b, h, i: (b, h)
    meta_map = lambda b, h, i: (0, h)
    return pl.pallas_call(
        functools.partial(_nbr_kernel, n_tiles=seq // NB_Q),
        grid=(batch, N_HEADS_B, nq),
        in_specs=[
            pl.BlockSpec((tqb, HEAD_DIM), qmap),
            pl.BlockSpec((seq, HEAD_DIM), kvmap),
            pl.BlockSpec((seq, HEAD_DIM), kvmap),
            pl.BlockSpec((META_PAD, HEAD_DIM), meta_map),
            pl.BlockSpec((META_PAD, HEAD_DIM), meta_map),
            pl.BlockSpec((1, NB_CLASSES, NB_Q, NB_K), lambda b, h, i: (h, 0, 0, 0)),
            pl.BlockSpec((tqb, HEAD_DIM), qmap),
        ],
        out_specs=pl.BlockSpec((tqb, HEAD_DIM), qmap),
        out_shape=jax.ShapeDtypeStruct((m, D_B), jnp.bfloat16),
        name="neighbourhood_attention",
        compiler_params=pltpu.CompilerParams(
            dimension_semantics=("arbitrary", "arbitrary", "arbitrary"),
            vmem_limit_bytes=VMEM_LIMIT),
    )(qb, kb, vb, km, vm, bias, zb)


def _outproj_kernel(x_ref, ya_ref, yb_ref, w_ref, o_ref):
    y = jnp.dot(ya_ref[...], w_ref[:D_A, :], preferred_element_type=jnp.float32)
    y = y + jnp.dot(yb_ref[...], w_ref[D_A:, :], preferred_element_type=jnp.float32)
    o_ref[...] = x_ref[...] + y


def _out_projection(x2d, ya, yb, w_out_bf16, tm):
    m = x2d.shape[0]
    row = lambda i: (i, 0)
    return pl.pallas_call(
        _outproj_kernel,
        grid=(m // tm,),
        in_specs=[
            pl.BlockSpec((tm, D_MODEL), row),
            pl.BlockSpec((tm, D_A), row),
            pl.BlockSpec((tm, D_B), row),
            pl.BlockSpec((D_A + D_B, D_MODEL), lambda i: (0, 0)),
        ],
        out_specs=pl.BlockSpec((tm, D_MODEL), row),
        out_shape=jax.ShapeDtypeStruct((m, D_MODEL), jnp.float32),
        name="out_projection",
        compiler_params=pltpu.CompilerParams(
            dimension_semantics=("arbitrary",),
            vmem_limit_bytes=VMEM_LIMIT),
    )(x2d, ya, yb, w_out_bf16)


def _rope_tables(row, col):
    half = HEAD_DIM // 2
    inv_freq = ROPE_THETA ** (-jnp.arange(0, half, 2, dtype=jnp.float32) / half)
    ang_r = row.astype(jnp.float32)[:, None] * inv_freq[None, :]
    ang_c = col.astype(jnp.float32)[:, None] * inv_freq[None, :]
    cr, sr, cc, sc = jnp.cos(ang_r), jnp.sin(ang_r), jnp.cos(ang_c), jnp.sin(ang_c)
    zero = jnp.zeros_like(sr)
    cos = jnp.concatenate([cr, cr, cc, cc], axis=-1)
    sin_a = jnp.concatenate([-sr, zero, -sc, zero], axis=-1)
    sin_b = jnp.concatenate([zero, sr, zero, sc], axis=-1)
    return cos, sin_a, sin_b


def _nbr_bias_tables(rpb):
    rows = 32
    tile_r = np.array([0, 2, 8, rows - 4, rows - 2])[:, None, None, None, None]
    dr = np.arange(TILES_PER_QUERY_ROWS)[None, :, None, None, None]
    c = np.arange(GRID_W)[None, None, :, None, None]
    jj = np.arange(NB_KROWS)[None, None, None, :, None]
    kc = np.arange(GRID_W)[None, None, None, None, :]
    k0 = np.clip(tile_r - WIN_R // 2, 0, rows - NB_KROWS)
    r = tile_r + dr
    kr = k0 + jj
    r0 = np.clip(r - WIN_R // 2, 0, rows - WIN_R)
    c0 = np.clip(c - WIN_C // 2, 0, GRID_W - WIN_C)
    valid = (kr >= r0) & (kr < r0 + WIN_R) & (kc >= c0) & (kc < c0 + WIN_C)
    valid = np.broadcast_to(valid, (NB_CLASSES, TILES_PER_QUERY_ROWS, GRID_W, NB_KROWS, GRID_W))
    valid = valid.reshape(NB_CLASSES, NB_Q, NB_K)
    off_r = np.clip(kr - r + (WIN_R - 1), 0, 2 * WIN_R - 2).reshape(-1)
    n_off_c = 2 * WIN_C - 1
    pos = np.arange(GRID_W)
    onehot = (pos[None, None, :] - pos[None, :, None] + (WIN_C - 1)
              == np.arange(n_off_c)[:, None, None]).astype(np.float32)
    toeplitz = jnp.einsum('hro,ock->hrck', rpb.astype(jnp.float32) * LOG2E, onehot,
                          precision=lax.Precision.HIGHEST)
    tab = jnp.take(toeplitz, off_r, axis=1)
    tab = tab.reshape(N_HEADS_B, NB_CLASSES, TILES_PER_QUERY_ROWS, NB_KROWS, GRID_W, GRID_W)
    tab = tab.transpose(0, 1, 2, 4, 3, 5).reshape(N_HEADS_B, NB_CLASSES, NB_Q, NB_K)
    return jnp.where(valid[None], tab, MASK_VALUE)


def kernel(x_prompt, x_sample, meta_tokens, norm_w, w_in, q_norm_a, k_norm_a, q_norm_b, k_norm_b, rpb, w_out):
    w_in_bf = w_in[0].astype(jnp.bfloat16)
    w_out_bf = w_out[0].astype(jnp.bfloat16)
    nw = norm_w[0].reshape(1, D_MODEL)
    gains = [g[0].reshape(1, HEAD_DIM) for g in (q_norm_a, k_norm_a, q_norm_b, k_norm_b)]
    bias = _nbr_bias_tables(rpb[0])

    meta_pos = jnp.arange(META_PAD, dtype=jnp.int32)
    meta_tabs = _rope_tables(jnp.full((META_PAD,), -1, jnp.int32), meta_pos)
    meta_x = jnp.pad(meta_tokens, ((0, META_PAD - N_META), (0, 0)))
    meta_out = _in_projection(meta_x, META_PAD, META_PAD, nw, w_in_bf, gains, meta_tabs)
    _, km_a, vmt_a, _, _, km_b, vm_b, _ = meta_out

    def encode(x):
        batch, seq, _ = x.shape
        x2d = x.reshape(batch * seq, D_MODEL)
        t = jnp.arange(seq, dtype=jnp.int32)
        tabs = _rope_tables(t // GRID_W, t % GRID_W)
        qa, ka, vat, za, qb, kb, vb, zb = _in_projection(x2d, seq, ROW_TILE, nw, w_in_bf, gains, tabs)
        ya = _global_attention(qa, ka, vat, za, km_a, vmt_a, batch, seq, GQA_Q_TILE)
        yb = _neighbourhood_attention(qb, kb, vb, zb, km_b, vm_b, bias, batch, seq, NB_ROWS_PER_STEP)
        y = _out_projection(x2d, ya, yb, w_out_bf, ROW_TILE)
        return y.reshape(batch, seq, D_MODEL)

    return (encode(x_prompt), encode(x_sample))
```

```python
import functools
import math

import jax
import jax.numpy as jnp
import numpy as np
from jax import lax
from jax.experimental import pallas as pl
from jax.experimental.pallas import tpu as pltpu

D_MODEL = 2048
HEAD_DIM = 128
N_HEADS_A = 8
N_KV_A = 2
GROUP_A = N_HEADS_A // N_KV_A
N_HEADS_B = 8
D_A = N_HEADS_A * HEAD_DIM
D_KV_A = N_KV_A * HEAD_DIM
D_B = N_HEADS_B * HEAD_DIM
D_IN = D_A + 2 * D_KV_A + D_A + 4 * D_B
N_META = 16
GRID_W = 64
WIN_R = 8
WIN_C = 16
ROPE_THETA = 10000.0
EPS = 1e-6

LOG2E = math.log2(math.e)
Q_SCALE = HEAD_DIM ** -0.5 * LOG2E
MASK_VALUE = -1e30

LANES = 128
META_PAD = LANES
VMEM_LIMIT = 56 * 1024 * 1024

TN = 512
N_COL_TILES = D_IN // TN
TILES_PER_QUERY_ROWS = 2
NB_Q = TILES_PER_QUERY_ROWS * GRID_W
NB_KROWS = WIN_R + TILES_PER_QUERY_ROWS - 1
NB_K = NB_KROWS * GRID_W
NB_CLASSES = 5

IN_ROW_TILE = 256
OUT_ROW_TILE = 512
GQA_KEY_CHUNK = 512
GQA_Q_TILE = 256
NB_ROWS_PER_STEP = 1024
NB_INTERLEAVE = 4


def _head_norm(a, gain):
    ms = jnp.mean(a * a, axis=-1, keepdims=True)
    return a * lax.rsqrt(ms + EPS) * gain


def _rope(y, cos, sin_a, sin_b):
    return y * cos + pltpu.roll(y, 96, 1) * sin_a + pltpu.roll(y, 32, 1) * sin_b


def _inproj_kernel(x_ref, nw_ref, w_ref, gqa_ref, gka_ref, gqb_ref, gkb_ref,
                   cos_ref, sa_ref, sb_ref,
                   qat_ref, ka_ref, vat_ref, za_ref, qb_ref, kb_ref, vb_ref, zb_ref,
                   xn_ref):
    x = x_ref[...]
    ms = jnp.mean(x * x, axis=-1, keepdims=True)
    xn_ref[...] = (x * lax.rsqrt(ms + EPS) * nw_ref[...]).astype(jnp.bfloat16)

    def column_tile(j):
        return jnp.dot(xn_ref[...], w_ref[:, j * TN:(j + 1) * TN],
                       preferred_element_type=jnp.float32)

    def normed_roped(a, gain_ref):
        return _rope(_head_norm(a, gain_ref[...]), cos_ref[...], sa_ref[...], sb_ref[...])

    def store_heads(ref, first, vals):
        for h, v in enumerate(vals):
            ref[:, (first + h) * HEAD_DIM:(first + h + 1) * HEAD_DIM] = v.astype(ref.dtype)

    heads_per_tile = TN // HEAD_DIM

    def epilogue(j, acc):
        heads = [acc[:, h * HEAD_DIM:(h + 1) * HEAD_DIM] for h in range(heads_per_tile)]
        if j < 2:
            for h, a in enumerate(heads):
                q = normed_roped(a, gqa_ref) * Q_SCALE
                qat_ref[j * heads_per_tile + h] = q.T.astype(qat_ref.dtype)
        elif j == 2:
            store_heads(ka_ref, 0, [normed_roped(a, gka_ref) for a in heads[:N_KV_A]])
            for g, a in enumerate(heads[N_KV_A:]):
                vat_ref[g, 0] = a.T.astype(vat_ref.dtype)
        elif j < 5:
            za_ref[:, (j - 3) * TN:(j - 2) * TN] = acc
        elif j < 7:
            store_heads(qb_ref, (j - 5) * heads_per_tile,
                        [_head_norm(a, gqb_ref[...]) * Q_SCALE for a in heads])
        elif j < 9:
            store_heads(kb_ref, (j - 7) * heads_per_tile,
                        [_head_norm(a, gkb_ref[...]) for a in heads])
        elif j < 11:
            vb_ref[:, (j - 9) * TN:(j - 8) * TN] = acc.astype(vb_ref.dtype)
        else:
            zb_ref[:, (j - 11) * TN:(j - 10) * TN] = acc

    acc = column_tile(0)
    for j in range(N_COL_TILES):
        nxt = column_tile(j + 1) if j + 1 < N_COL_TILES else None
        epilogue(j, acc)
        acc = nxt


def _in_projection(x2d, seq, tm, chunk, norm_w, w_in_bf16, gains, tabs):
    m = x2d.shape[0]
    n_tab = seq // tm
    sub = chunk // tm
    cos, sin_a, sin_b = tabs
    row = lambda i: (i, 0)
    const = lambda i: (0, 0)
    bf, f32 = jnp.bfloat16, jnp.float32
    out_shape = [
        jax.ShapeDtypeStruct((N_HEADS_A, HEAD_DIM, m), bf), jax.ShapeDtypeStruct((m, D_KV_A), bf),
        jax.ShapeDtypeStruct((N_KV_A, m // chunk, HEAD_DIM, chunk), bf),
        jax.ShapeDtypeStruct((m, D_A), f32),
        jax.ShapeDtypeStruct((m, D_B), bf), jax.ShapeDtypeStruct((m, D_B), bf),
        jax.ShapeDtypeStruct((m, D_B), bf), jax.ShapeDtypeStruct((m, D_B), f32),
    ]
    out_specs = [
        pl.BlockSpec((N_HEADS_A, HEAD_DIM, tm), lambda i: (0, 0, i)),
        pl.BlockSpec((tm, D_KV_A), row),
        pl.BlockSpec((N_KV_A, 1, HEAD_DIM, tm), lambda i: (0, i // sub, 0, i % sub)),
        pl.BlockSpec((tm, D_A), row),
        pl.BlockSpec((tm, D_B), row), pl.BlockSpec((tm, D_B), row),
        pl.BlockSpec((tm, D_B), row), pl.BlockSpec((tm, D_B), row),
    ]
    gain_spec = pl.BlockSpec((1, HEAD_DIM), const)
    tab_spec = pl.BlockSpec((tm, HEAD_DIM), lambda i: (i % n_tab, 0))
    return pl.pallas_call(
        _inproj_kernel,
        grid=(m // tm,),
        in_specs=[
            pl.BlockSpec((tm, D_MODEL), row),
            pl.BlockSpec((1, D_MODEL), const),
            pl.BlockSpec((D_MODEL, D_IN), const, pipeline_mode=pl.Buffered(1)),
            gain_spec, gain_spec, gain_spec, gain_spec,
            tab_spec, tab_spec, tab_spec,
        ],
        out_specs=out_specs,
        out_shape=out_shape,
        scratch_shapes=[pltpu.VMEM((tm, D_MODEL), bf)],
        name="in_projection",
        compiler_params=pltpu.CompilerParams(
            dimension_semantics=("arbitrary",),
            vmem_limit_bytes=VMEM_LIMIT),
    )(x2d, norm_w, w_in_bf16, *gains, cos, sin_a, sin_b)


def _nt_dot(a, b):
    return lax.dot_general(a, b, (((1,), (1,)), ((), ())), preferred_element_type=jnp.float32)


def _silu(z):
    return z * (1.0 / (1.0 + jnp.exp(-z)))


GQA_PAIRS = GROUP_A // 2


def _gqa_kernel(qt_ref, k_ref, vt_ref, km_ref, vmt_ref, z_ref, o_ref, acc_ref, s_ref):
    tq = qt_ref.shape[2]
    n_chunks, chunk = vt_ref.shape[1], vt_ref.shape[3]
    width = 2 * tq
    row = lax.broadcasted_iota(jnp.int32, (META_PAD, width), 0)
    meta_bias = jnp.where(row < N_META, 0.0, MASK_VALUE).astype(jnp.float32)
    km = km_ref[...]
    vmt = vmt_ref[0, 0]
    qts = [jnp.concatenate([qt_ref[2 * i], qt_ref[2 * i + 1]], axis=1) for i in range(GQA_PAIRS)]

    def dot(a, b):
        return jnp.dot(a, b, preferred_element_type=jnp.float32)

    carry = []
    for i, qt in enumerate(qts):
        s = dot(km, qt) + meta_bias
        m = jnp.max(s, axis=0, keepdims=True)
        p = jnp.exp2(s - m)
        carry += [m, jnp.sum(p, axis=0, keepdims=True)]
        acc_ref[i] = dot(vmt, p.astype(jnp.bfloat16))

    def softmax_pv(i, s, m, l, vt):
        m_new = jnp.maximum(m, jnp.max(s, axis=0, keepdims=True))
        alpha = jnp.exp2(m - m_new)
        p = jnp.exp2(s - m_new)
        l = alpha * l + jnp.sum(p, axis=0, keepdims=True)
        acc_ref[i] = alpha * acc_ref[i] + dot(vt, p.astype(jnp.bfloat16))
        return m_new, l

    def keys(c):
        return k_ref[pl.ds(pl.multiple_of(c * chunk, chunk), chunk), :]

    s_ref[0] = dot(keys(0), qts[0])

    def body(c2, carry):
        m0, l0, m1, l1 = carry
        c = 2 * c2
        for slot in range(2):
            cur = c + slot
            nxt = jnp.minimum(cur + 1, n_chunks - 1)
            k = keys(cur)
            vt = vt_ref[0, cur]
            s1 = dot(k, qts[1])
            m0, l0 = softmax_pv(0, s_ref[slot], m0, l0, vt)
            s_ref[1 - slot] = dot(keys(nxt), qts[0])
            m1, l1 = softmax_pv(1, s1, m1, l1, vt)
        return m0, l0, m1, l1

    carry = lax.fori_loop(0, n_chunks // 2, body, tuple(carry))
    for i in range(GQA_PAIRS):
        o_pair = acc_ref[i] * (1.0 / carry[2 * i + 1])
        for j in range(2):
            h = 2 * i + j
            sl = slice(h * HEAD_DIM, (h + 1) * HEAD_DIM)
            o = o_pair[:, j * tq:(j + 1) * tq].T
            o_ref[:, sl] = (o * _silu(z_ref[:, sl])).astype(o_ref.dtype)


def _global_attention(qat, ka, vat, za, km, vmt, batch, seq, tq):
    m = za.shape[0]
    nq = seq // tq
    chunk = vat.shape[3]
    n_chunks = seq // chunk
    qmap = lambda b, g, i: (b * nq + i, g)
    width = GROUP_A * HEAD_DIM
    return pl.pallas_call(
        _gqa_kernel,
        grid=(batch, N_KV_A, nq),
        in_specs=[
            pl.BlockSpec((GROUP_A, HEAD_DIM, tq), lambda b, g, i: (g, 0, b * nq + i)),
            pl.BlockSpec((seq, HEAD_DIM), lambda b, g, i: (b, g)),
            pl.BlockSpec((1, n_chunks, HEAD_DIM, chunk), lambda b, g, i: (g, b, 0, 0)),
            pl.BlockSpec((META_PAD, HEAD_DIM), lambda b, g, i: (0, g)),
            pl.BlockSpec((1, 1, HEAD_DIM, META_PAD), lambda b, g, i: (g, 0, 0, 0)),
            pl.BlockSpec((tq, width), qmap),
        ],
        out_specs=pl.BlockSpec((tq, width), qmap),
        out_shape=jax.ShapeDtypeStruct((m, D_A), jnp.bfloat16),
        scratch_shapes=[pltpu.VMEM((GQA_PAIRS, HEAD_DIM, 2 * tq), jnp.float32),
                        pltpu.VMEM((2, chunk, 2 * tq), jnp.float32)],
        name="global_attention",
        compiler_params=pltpu.CompilerParams(
            dimension_semantics=("arbitrary", "arbitrary", "arbitrary"),
            vmem_limit_bytes=VMEM_LIMIT),
    )(qat, ka, vat, km, vmt, za)


def _nbr_kernel(q_ref, k_ref, v_ref, km_ref, vm_ref, bias_ref, z_ref, o_ref, *, n_tiles):
    tiles_per_step = q_ref.shape[0] // NB_Q
    n_rows = n_tiles * TILES_PER_QUERY_ROWS
    lane = lax.broadcasted_iota(jnp.int32, (NB_Q, META_PAD), 1)
    meta_bias = jnp.where(lane < N_META, 0.0, MASK_VALUE).astype(jnp.float32)
    km = km_ref[...]
    vm = vm_ref[...]
    step = pl.program_id(2)

    def body(i, _):
        tiles = []
        for u in range(NB_INTERLEAVE):
            local = i * NB_INTERLEAVE + u
            t = step * tiles_per_step + local
            cls = jnp.where(t < 2, t, jnp.where(t >= n_tiles - 2, t - n_tiles + NB_CLASSES, 2))
            k0 = jnp.clip(t * TILES_PER_QUERY_ROWS - WIN_R // 2, 0, n_rows - NB_KROWS)
            kstart = pl.multiple_of(k0 * GRID_W, GRID_W)
            qrows = pl.ds(pl.multiple_of(local * NB_Q, NB_Q), NB_Q)
            tiles.append((cls, kstart, qrows))
        scores = []
        for cls, kstart, qrows in tiles:
            q = q_ref[qrows, :]
            s_meta = _nt_dot(q, km) + meta_bias
            s_win = _nt_dot(q, k_ref[pl.ds(kstart, NB_K), :]) + bias_ref[0, cls]
            scores.append((s_meta, s_win))
        probs = []
        for s_meta, s_win in scores:
            m = jnp.maximum(jnp.max(s_meta, axis=-1, keepdims=True),
                            jnp.max(s_win, axis=-1, keepdims=True))
            p_meta = jnp.exp2(s_meta - m)
            p_win = jnp.exp2(s_win - m)
            l = jnp.sum(p_meta, axis=-1, keepdims=True) + jnp.sum(p_win, axis=-1, keepdims=True)
            probs.append((p_meta.astype(jnp.bfloat16), p_win.astype(jnp.bfloat16), l))
        for (cls, kstart, qrows), (p_meta, p_win, l) in zip(tiles, probs):
            v = v_ref[pl.ds(kstart, NB_K), :]
            acc = (jnp.dot(p_meta, vm, preferred_element_type=jnp.float32)
                   + jnp.dot(p_win, v, preferred_element_type=jnp.float32))
            o_ref[qrows, :] = (acc * (1.0 / l) * _silu(z_ref[qrows, :])).astype(o_ref.dtype)
        return 0

    lax.fori_loop(0, tiles_per_step // NB_INTERLEAVE, body, 0)


def _neighbourhood_attention(qb, kb, vb, zb, km, vm, bias, batch, seq, tqb):
    m = qb.shape[0]
    nq = seq // tqb
    qmap = lambda b, h, i: (b * nq + i, h)
    kvmap = lambda b, h, i: (b, h)
    meta_map = lambda b, h, i: (0, h)
    return pl.pallas_call(
        functools.partial(_nbr_kernel, n_tiles=seq // NB_Q),
        grid=(batch, N_HEADS_B, nq),
        in_specs=[
            pl.BlockSpec((tqb, HEAD_DIM), qmap),
            pl.BlockSpec((seq, HEAD_DIM), kvmap),
            pl.BlockSpec((seq, HEAD_DIM), kvmap),
            pl.BlockSpec((META_PAD, HEAD_DIM), meta_map),
            pl.BlockSpec((META_PAD, HEAD_DIM), meta_map),
            pl.BlockSpec((1, NB_CLASSES, NB_Q, NB_K), lambda b, h, i: (h, 0, 0, 0)),
            pl.BlockSpec((tqb, HEAD_DIM), qmap),
        ],
        out_specs=pl.BlockSpec((tqb, HEAD_DIM), qmap),
        out_shape=jax.ShapeDtypeStruct((m, D_B), jnp.bfloat16),
        name="neighbourhood_attention",
        compiler_params=pltpu.CompilerParams(
            dimension_semantics=("arbitrary", "arbitrary", "arbitrary"),
            vmem_limit_bytes=VMEM_LIMIT),
    )(qb, kb, vb, km, vm, bias, zb)


def _outproj_kernel(x_ref, ya_ref, yb_ref, w_ref, o_ref):
    y = jnp.dot(ya_ref[...], w_ref[:D_A, :], preferred_element_type=jnp.float32)
    y = y + jnp.dot(yb_ref[...], w_ref[D_A:, :], preferred_element_type=jnp.float32)
    o_ref[...] = x_ref[...] + y


def _out_projection(x2d, ya, yb, w_out_bf16, tm):
    m = x2d.shape[0]
    row = lambda i: (i, 0)
    return pl.pallas_call(
        _outproj_kernel,
        grid=(m // tm,),
        in_specs=[
            pl.BlockSpec((tm, D_MODEL), row),
            pl.BlockSpec((tm, D_A), row),
            pl.BlockSpec((tm, D_B), row),
            pl.BlockSpec((D_A + D_B, D_MODEL), lambda i: (0, 0)),
        ],
        out_specs=pl.BlockSpec((tm, D_MODEL), row),
        out_shape=jax.ShapeDtypeStruct((m, D_MODEL), jnp.float32),
        name="out_projection",
        compiler_params=pltpu.CompilerParams(
            dimension_semantics=("arbitrary",),
            vmem_limit_bytes=VMEM_LIMIT),
    )(x2d, ya, yb, w_out_bf16)


def _rope_tables(row, col):
    half = HEAD_DIM // 2
    inv_freq = ROPE_THETA ** (-jnp.arange(0, half, 2, dtype=jnp.float32) / half)
    ang_r = row.astype(jnp.float32)[:, None] * inv_freq[None, :]
    ang_c = col.astype(jnp.float32)[:, None] * inv_freq[None, :]
    cr, sr, cc, sc = jnp.cos(ang_r), jnp.sin(ang_r), jnp.cos(ang_c), jnp.sin(ang_c)
    zero = jnp.zeros_like(sr)
    cos = jnp.concatenate([cr, cr, cc, cc], axis=-1)
    sin_a = jnp.concatenate([-sr, zero, -sc, zero], axis=-1)
    sin_b = jnp.concatenate([zero, sr, zero, sc], axis=-1)
    return cos, sin_a, sin_b


def _nbr_bias_tables(rpb):
    rows = 32
    tile_r = np.array([0, 2, 8, rows - 4, rows - 2])[:, None, None, None, None]
    dr = np.arange(TILES_PER_QUERY_ROWS)[None, :, None, None, None]
    c = np.arange(GRID_W)[None, None, :, None, None]
    jj = np.arange(NB_KROWS)[None, None, None, :, None]
    kc = np.arange(GRID_W)[None, None, None, None, :]
    k0 = np.clip(tile_r - WIN_R // 2, 0, rows - NB_KROWS)
    r = tile_r + dr
    kr = k0 + jj
    r0 = np.clip(r - WIN_R // 2, 0, rows - WIN_R)
    c0 = np.clip(c - WIN_C // 2, 0, GRID_W - WIN_C)
    valid = (kr >= r0) & (kr < r0 + WIN_R) & (kc >= c0) & (kc < c0 + WIN_C)
    valid = np.broadcast_to(valid, (NB_CLASSES, TILES_PER_QUERY_ROWS, GRID_W, NB_KROWS, GRID_W))
    valid = valid.reshape(NB_CLASSES, NB_Q, NB_K)
    off_r = np.clip(kr - r + (WIN_R - 1), 0, 2 * WIN_R - 2).reshape(-1)
    n_off_c = 2 * WIN_C - 1
    pos = np.arange(GRID_W)
    onehot = (pos[None, None, :] - pos[None, :, None] + (WIN_C - 1)
              == np.arange(n_off_c)[:, None, None]).astype(np.float32)
    toeplitz = jnp.einsum('hro,ock->hrck', rpb.astype(jnp.float32) * LOG2E, onehot,
                          precision=lax.Precision.HIGHEST)
    tab = jnp.take(toeplitz, off_r, axis=1)
    tab = tab.reshape(N_HEADS_B, NB_CLASSES, TILES_PER_QUERY_ROWS, NB_KROWS, GRID_W, GRID_W)
    tab = tab.transpose(0, 1, 2, 4, 3, 5).reshape(N_HEADS_B, NB_CLASSES, NB_Q, NB_K)
    return jnp.where(valid[None], tab, MASK_VALUE)


def kernel(x_prompt, x_sample, meta_tokens, norm_w, w_in, q_norm_a, k_norm_a, q_norm_b, k_norm_b, rpb, w_out):
    w_in_bf = w_in[0].astype(jnp.bfloat16)
    w_out_bf = w_out[0].astype(jnp.bfloat16)
    nw = norm_w[0].reshape(1, D_MODEL)
    gains = [g[0].reshape(1, HEAD_DIM) for g in (q_norm_a, k_norm_a, q_norm_b, k_norm_b)]
    bias = _nbr_bias_tables(rpb[0])

    meta_pos = jnp.arange(META_PAD, dtype=jnp.int32)
    meta_tabs = _rope_tables(jnp.full((META_PAD,), -1, jnp.int32), meta_pos)
    meta_x = jnp.pad(meta_tokens, ((0, META_PAD - N_META), (0, 0)))
    meta_out = _in_projection(meta_x, META_PAD, META_PAD, META_PAD, nw, w_in_bf, gains, meta_tabs)
    _, km_a, vmt_a, _, _, km_b, vm_b, _ = meta_out

    def encode(x):
        batch, seq, _ = x.shape
        x2d = x.reshape(batch * seq, D_MODEL)
        t = jnp.arange(seq, dtype=jnp.int32)
        tabs = _rope_tables(t // GRID_W, t % GRID_W)
        qat, ka, vat, za, qb, kb, vb, zb = _in_projection(
            x2d, seq, IN_ROW_TILE, GQA_KEY_CHUNK, nw, w_in_bf, gains, tabs)
        ya = _global_attention(qat, ka, vat, za, km_a, vmt_a, batch, seq, GQA_Q_TILE)
        yb = _neighbourhood_attention(qb, kb, vb, zb, km_b, vm_b, bias, batch, seq, NB_ROWS_PER_STEP)
        y = _out_projection(x2d, ya, yb, w_out_bf, OUT_ROW_TILE)
        return y.reshape(batch, seq, D_MODEL)

    return (encode(x_prompt), encode(x_sample))
```

```python
import functools
import math

import jax
import jax.numpy as jnp
import numpy as np
from jax import lax
from jax.experimental import pallas as pl
from jax.experimental.pallas import tpu as pltpu

D_MODEL = 2048
HEAD_DIM = 128
N_HEADS_A = 8
N_KV_A = 2
GROUP_A = N_HEADS_A // N_KV_A
N_HEADS_B = 8
D_A = N_HEADS_A * HEAD_DIM
D_KV_A = N_KV_A * HEAD_DIM
D_B = N_HEADS_B * HEAD_DIM
D_IN = D_A + 2 * D_KV_A + D_A + 4 * D_B
N_META = 16
GRID_W = 64
WIN_R = 8
WIN_C = 16
ROPE_THETA = 10000.0
EPS = 1e-6

LOG2E = math.log2(math.e)
Q_SCALE = HEAD_DIM ** -0.5 * LOG2E
MASK_VALUE = -1e30

LANES = 128
BF16_SUBLANE_TILE = 16
VT_EXTRA_ROWS = BF16_SUBLANE_TILE
VT_ROWS = HEAD_DIM + VT_EXTRA_ROWS
SAFE_SCORE_BOUND = 60.0
META_PAD = LANES
VMEM_LIMIT = 56 * 1024 * 1024

TN = 512
N_COL_TILES = D_IN // TN
TILES_PER_QUERY_ROWS = 2
NB_Q = TILES_PER_QUERY_ROWS * GRID_W
NB_KROWS = WIN_R + TILES_PER_QUERY_ROWS - 1
NB_K = NB_KROWS * GRID_W
NB_CLASSES = 5

IN_ROW_TILE = 256
OUT_ROW_TILE = 512
GQA_KEY_CHUNK = 512
GQA_Q_TILE = 256
NB_ROWS_PER_STEP = 1024
NB_INTERLEAVE = 8


def _silu(z):
    return z * (1.0 / (1.0 + jnp.exp(-z)))


def _head_norm(a, gain):
    ms = jnp.mean(a * a, axis=-1, keepdims=True)
    return a * lax.rsqrt(ms + EPS) * gain


def _rope(y, cos, sin_a, sin_b):
    return y * cos + pltpu.roll(y, 96, 1) * sin_a + pltpu.roll(y, 32, 1) * sin_b


def _inproj_kernel(x_ref, nw_ref, w_ref, gqa_ref, gka_ref, gqb_ref, gkb_ref,
                   cos_ref, sa_ref, sb_ref,
                   qat_ref, ka_ref, vat_ref, za_ref, qb_ref, kb_ref, vb_ref, zb_ref,
                   xn_ref):
    x = x_ref[...]
    ms = jnp.mean(x * x, axis=-1, keepdims=True)
    xn_ref[...] = (x * lax.rsqrt(ms + EPS) * nw_ref[...]).astype(jnp.bfloat16)

    def column_tile(j):
        return jnp.dot(xn_ref[...], w_ref[:, j * TN:(j + 1) * TN],
                       preferred_element_type=jnp.float32)

    def normed_roped(a, gain_ref):
        return _rope(_head_norm(a, gain_ref[...]), cos_ref[...], sa_ref[...], sb_ref[...])

    def store_heads(ref, first, vals):
        for h, v in enumerate(vals):
            ref[:, (first + h) * HEAD_DIM:(first + h + 1) * HEAD_DIM] = v.astype(ref.dtype)

    heads_per_tile = TN // HEAD_DIM

    def epilogue(j, acc):
        heads = [acc[:, h * HEAD_DIM:(h + 1) * HEAD_DIM] for h in range(heads_per_tile)]
        if j < 2:
            for h, a in enumerate(heads):
                q = normed_roped(a, gqa_ref) * Q_SCALE
                qat_ref[j * heads_per_tile + h] = q.T.astype(qat_ref.dtype)
        elif j == 2:
            store_heads(ka_ref, 0, [normed_roped(a, gka_ref) for a in heads[:N_KV_A]])
            ones_row = (lax.broadcasted_iota(jnp.int32, (VT_EXTRA_ROWS, acc.shape[0]), 0) == 0)
            for g, a in enumerate(heads[N_KV_A:]):
                vat_ref[g, 0, :HEAD_DIM, :] = a.T.astype(vat_ref.dtype)
                vat_ref[g, 0, HEAD_DIM:, :] = ones_row.astype(vat_ref.dtype)
        elif j < 5:
            za_ref[:, (j - 3) * TN:(j - 2) * TN] = _silu(acc)
        elif j < 7:
            store_heads(qb_ref, (j - 5) * heads_per_tile,
                        [_head_norm(a, gqb_ref[...]) * Q_SCALE for a in heads])
        elif j < 9:
            store_heads(kb_ref, (j - 7) * heads_per_tile,
                        [_head_norm(a, gkb_ref[...]) for a in heads])
        elif j < 11:
            vb_ref[:, (j - 9) * TN:(j - 8) * TN] = acc.astype(vb_ref.dtype)
        else:
            zb_ref[:, (j - 11) * TN:(j - 10) * TN] = _silu(acc)

    acc = column_tile(0)
    for j in range(N_COL_TILES):
        nxt = column_tile(j + 1) if j + 1 < N_COL_TILES else None
        epilogue(j, acc)
        acc = nxt


def _in_projection(x2d, seq, tm, chunk, norm_w, w_in_bf16, gains, tabs):
    m = x2d.shape[0]
    n_tab = seq // tm
    sub = chunk // tm
    cos, sin_a, sin_b = tabs
    row = lambda i: (i, 0)
    const = lambda i: (0, 0)
    bf, f32 = jnp.bfloat16, jnp.float32
    out_shape = [
        jax.ShapeDtypeStruct((N_HEADS_A, HEAD_DIM, m), bf), jax.ShapeDtypeStruct((m, D_KV_A), bf),
        jax.ShapeDtypeStruct((N_KV_A, m // chunk, VT_ROWS, chunk), bf),
        jax.ShapeDtypeStruct((m, D_A), f32),
        jax.ShapeDtypeStruct((m, D_B), bf), jax.ShapeDtypeStruct((m, D_B), bf),
        jax.ShapeDtypeStruct((m, D_B), bf), jax.ShapeDtypeStruct((m, D_B), f32),
    ]
    out_specs = [
        pl.BlockSpec((N_HEADS_A, HEAD_DIM, tm), lambda i: (0, 0, i)),
        pl.BlockSpec((tm, D_KV_A), row),
        pl.BlockSpec((N_KV_A, 1, VT_ROWS, tm), lambda i: (0, i // sub, 0, i % sub)),
        pl.BlockSpec((tm, D_A), row),
        pl.BlockSpec((tm, D_B), row), pl.BlockSpec((tm, D_B), row),
        pl.BlockSpec((tm, D_B), row), pl.BlockSpec((tm, D_B), row),
    ]
    gain_spec = pl.BlockSpec((1, HEAD_DIM), const)
    tab_spec = pl.BlockSpec((tm, HEAD_DIM), lambda i: (i % n_tab, 0))
    return pl.pallas_call(
        _inproj_kernel,
        grid=(m // tm,),
        in_specs=[
            pl.BlockSpec((tm, D_MODEL), row),
            pl.BlockSpec((1, D_MODEL), const),
            pl.BlockSpec((D_MODEL, D_IN), const, pipeline_mode=pl.Buffered(1)),
            gain_spec, gain_spec, gain_spec, gain_spec,
            tab_spec, tab_spec, tab_spec,
        ],
        out_specs=out_specs,
        out_shape=out_shape,
        scratch_shapes=[pltpu.VMEM((tm, D_MODEL), bf)],
        name="in_projection",
        compiler_params=pltpu.CompilerParams(
            dimension_semantics=("arbitrary",),
            vmem_limit_bytes=VMEM_LIMIT),
    )(x2d, norm_w, w_in_bf16, *gains, cos, sin_a, sin_b)


def _nt_dot(a, b):
    return lax.dot_general(a, b, (((1,), (1,)), ((), ())), preferred_element_type=jnp.float32)


GQA_PAIRS = GROUP_A // 2


def _gqa_kernel(safe_ref, qt_ref, k_ref, vt_ref, km_ref, vmt_ref, g_ref, o_ref, acc_ref, s_ref):
    tq = qt_ref.shape[2]
    n_chunks, chunk = vt_ref.shape[1], vt_ref.shape[3]
    width = 2 * tq
    row = lax.broadcasted_iota(jnp.int32, (META_PAD, width), 0)
    meta_bias = jnp.where(row < N_META, 0.0, MASK_VALUE).astype(jnp.float32)
    km = km_ref[...]
    vmt = vmt_ref[0, 0]
    qts = [jnp.concatenate([qt_ref[2 * i], qt_ref[2 * i + 1]], axis=1) for i in range(GQA_PAIRS)]

    def dot(a, b):
        return jnp.dot(a, b, preferred_element_type=jnp.float32)

    def keys(c):
        return k_ref[pl.ds(pl.multiple_of(c * chunk, chunk), chunk), :]

    def attend(init, consume):
        s_ref[0] = dot(keys(0), qts[0])
        s_meta = [dot(km, qt) + meta_bias for qt in qts]
        state = tuple(init(i, s) for i, s in enumerate(s_meta))

        def body(c2, state):
            st0, st1 = state
            for slot in range(2):
                cur = 2 * c2 + slot
                nxt = jnp.minimum(cur + 1, n_chunks - 1)
                vt = vt_ref[0, cur]
                s1 = dot(keys(cur), qts[1])
                st0 = consume(0, s_ref[slot], vt, st0)
                s_ref[1 - slot] = dot(keys(nxt), qts[0])
                st1 = consume(1, s1, vt, st1)
            return st0, st1

        lax.fori_loop(0, n_chunks // 2, body, state)
        for i in range(GQA_PAIRS):
            acc = acc_ref[i]
            o_pair = acc[:HEAD_DIM] * (1.0 / acc[HEAD_DIM:HEAD_DIM + 1])
            for j in range(2):
                h = 2 * i + j
                sl = slice(h * HEAD_DIM, (h + 1) * HEAD_DIM)
                o = o_pair[:, j * tq:(j + 1) * tq].T
                o_ref[:, sl] = (o * g_ref[:, sl]).astype(o_ref.dtype)

    safe = safe_ref[0] != 0

    @pl.when(safe)
    def _():
        def init(i, s):
            acc_ref[i] = dot(vmt, jnp.exp2(s).astype(jnp.bfloat16))
            return 0

        def consume(i, s, vt, state):
            acc_ref[i] += dot(vt, jnp.exp2(s).astype(jnp.bfloat16))
            return state

        attend(init, consume)

    @pl.when(jnp.logical_not(safe))
    def _():
        def init(i, s):
            m = jnp.max(s, axis=0, keepdims=True)
            acc_ref[i] = dot(vmt, jnp.exp2(s - m).astype(jnp.bfloat16))
            return m

        def consume(i, s, vt, m):
            m_new = jnp.maximum(m, jnp.max(s, axis=0, keepdims=True))
            p = jnp.exp2(s - m_new)
            acc_ref[i] = jnp.exp2(m - m_new) * acc_ref[i] + dot(vt, p.astype(jnp.bfloat16))
            return m_new

        attend(init, consume)


def _global_attention(safe, qat, ka, vat, gate, km, vmt, batch, seq, tq):
    m = gate.shape[0]
    nq = seq // tq
    chunk = vat.shape[3]
    n_chunks = seq // chunk
    qmap = lambda b, g, i: (b * nq + i, g)
    width = GROUP_A * HEAD_DIM
    return pl.pallas_call(
        _gqa_kernel,
        grid=(batch, N_KV_A, nq),
        in_specs=[
            pl.BlockSpec(memory_space=pltpu.SMEM),
            pl.BlockSpec((GROUP_A, HEAD_DIM, tq), lambda b, g, i: (g, 0, b * nq + i)),
            pl.BlockSpec((seq, HEAD_DIM), lambda b, g, i: (b, g)),
            pl.BlockSpec((1, n_chunks, VT_ROWS, chunk), lambda b, g, i: (g, b, 0, 0)),
            pl.BlockSpec((META_PAD, HEAD_DIM), lambda b, g, i: (0, g)),
            pl.BlockSpec((1, 1, VT_ROWS, META_PAD), lambda b, g, i: (g, 0, 0, 0)),
            pl.BlockSpec((tq, width), qmap),
        ],
        out_specs=pl.BlockSpec((tq, width), qmap),
        out_shape=jax.ShapeDtypeStruct((m, D_A), jnp.bfloat16),
        scratch_shapes=[pltpu.VMEM((GQA_PAIRS, VT_ROWS, 2 * tq), jnp.float32),
                        pltpu.VMEM((2, chunk, 2 * tq), jnp.float32)],
        name="global_attention",
        compiler_params=pltpu.CompilerParams(
            dimension_semantics=("arbitrary", "arbitrary", "arbitrary"),
            vmem_limit_bytes=VMEM_LIMIT),
    )(safe, qat, ka, vat, km, vmt, gate)


def _nbr_kernel(safe_ref, q_ref, k_ref, v_ref, km_ref, vm_ref, bias_ref, g_ref, o_ref, *, n_tiles):
    tiles_per_step = q_ref.shape[0] // NB_Q
    n_rows = n_tiles * TILES_PER_QUERY_ROWS
    lane = lax.broadcasted_iota(jnp.int32, (NB_Q, META_PAD), 1)
    meta_bias = jnp.where(lane < N_META, 0.0, MASK_VALUE).astype(jnp.float32)
    km = km_ref[...]
    vm = vm_ref[...]
    step = pl.program_id(2)
    bf = jnp.bfloat16

    def dot(a, b):
        return jnp.dot(a, b, preferred_element_type=jnp.float32)

    def ones_column(n):
        return (lax.broadcasted_iota(jnp.int32, (n, LANES), 1) == 0).astype(bf)

    def make_body(probs_fn, with_ones):
        def body(i, _):
            tiles = []
            for u in range(NB_INTERLEAVE):
                local = i * NB_INTERLEAVE + u
                t = step * tiles_per_step + local
                cls = jnp.where(t < 2, t, jnp.where(t >= n_tiles - 2, t - n_tiles + NB_CLASSES, 2))
                k0 = jnp.clip(t * TILES_PER_QUERY_ROWS - WIN_R // 2, 0, n_rows - NB_KROWS)
                kstart = pl.multiple_of(k0 * GRID_W, GRID_W)
                qrows = pl.ds(pl.multiple_of(local * NB_Q, NB_Q), NB_Q)
                tiles.append((cls, kstart, qrows))
            scores = []
            for cls, kstart, qrows in tiles:
                q = q_ref[qrows, :]
                s_meta = _nt_dot(q, km) + meta_bias
                s_win = _nt_dot(q, k_ref[pl.ds(kstart, NB_K), :]) + bias_ref[0, cls]
                scores.append((s_meta, s_win))
            probs = [probs_fn(s_meta, s_win) for s_meta, s_win in scores]
            for (cls, kstart, qrows), (p_meta, p_win, l) in zip(tiles, probs):
                v = v_ref[pl.ds(kstart, NB_K), :]
                v_meta = vm
                if with_ones:
                    v = jnp.concatenate([v, ones_column(NB_K)], axis=1)
                    v_meta = jnp.concatenate([vm, ones_column(META_PAD)], axis=1)
                acc = dot(p_meta, v_meta) + dot(p_win, v)
                if with_ones:
                    l = acc[:, HEAD_DIM:HEAD_DIM + 1]
                    acc = acc[:, :HEAD_DIM]
                o_ref[qrows, :] = (acc * (1.0 / l) * g_ref[qrows, :]).astype(o_ref.dtype)
            return 0
        return body

    def probs_safe(s_meta, s_win):
        return jnp.exp2(s_meta).astype(bf), jnp.exp2(s_win).astype(bf), None

    def probs_online_max(s_meta, s_win):
        m = jnp.maximum(jnp.max(s_meta, axis=-1, keepdims=True),
                        jnp.max(s_win, axis=-1, keepdims=True))
        p_meta = jnp.exp2(s_meta - m)
        p_win = jnp.exp2(s_win - m)
        l = jnp.sum(p_meta, axis=-1, keepdims=True) + jnp.sum(p_win, axis=-1, keepdims=True)
        return p_meta.astype(bf), p_win.astype(bf), l

    safe = safe_ref[0] != 0
    n_trips = tiles_per_step // NB_INTERLEAVE

    @pl.when(safe)
    def _():
        lax.fori_loop(0, n_trips, make_body(probs_safe, True), 0)

    @pl.when(jnp.logical_not(safe))
    def _():
        lax.fori_loop(0, n_trips, make_body(probs_online_max, False), 0)


def _neighbourhood_attention(safe, qb, kb, vb, gate, km, vm, bias, batch, seq, tqb):
    m = qb.shape[0]
    nq = seq // tqb
    qmap = lambda b, h, i: (b * nq + i, h)
    kvmap = lambda b, h, i: (b, h)
    meta_map = lambda b, h, i: (0, h)
    return pl.pallas_call(
        functools.partial(_nbr_kernel, n_tiles=seq // NB_Q),
        grid=(batch, N_HEADS_B, nq),
        in_specs=[
            pl.BlockSpec(memory_space=pltpu.SMEM),
            pl.BlockSpec((tqb, HEAD_DIM), qmap),
            pl.BlockSpec((seq, HEAD_DIM), kvmap),
            pl.BlockSpec((seq, HEAD_DIM), kvmap),
            pl.BlockSpec((META_PAD, HEAD_DIM), meta_map),
            pl.BlockSpec((META_PAD, HEAD_DIM), meta_map),
            pl.BlockSpec((1, NB_CLASSES, NB_Q, NB_K), lambda b, h, i: (h, 0, 0, 0)),
            pl.BlockSpec((tqb, HEAD_DIM), qmap),
        ],
        out_specs=pl.BlockSpec((tqb, HEAD_DIM), qmap),
        out_shape=jax.ShapeDtypeStruct((m, D_B), jnp.bfloat16),
        name="neighbourhood_attention",
        compiler_params=pltpu.CompilerParams(
            dimension_semantics=("arbitrary", "arbitrary", "arbitrary"),
            vmem_limit_bytes=VMEM_LIMIT),
    )(safe, qb, kb, vb, km, vm, bias, gate)


def _outproj_kernel(x_ref, ya_ref, yb_ref, w_ref, o_ref):
    y = jnp.dot(ya_ref[...], w_ref[:D_A, :], preferred_element_type=jnp.float32)
    y = y + jnp.dot(yb_ref[...], w_ref[D_A:, :], preferred_element_type=jnp.float32)
    o_ref[...] = x_ref[...] + y


def _out_projection(x2d, ya, yb, w_out_bf16, tm):
    m = x2d.shape[0]
    row = lambda i: (i, 0)
    return pl.pallas_call(
        _outproj_kernel,
        grid=(m // tm,),
        in_specs=[
            pl.BlockSpec((tm, D_MODEL), row),
            pl.BlockSpec((tm, D_A), row),
            pl.BlockSpec((tm, D_B), row),
            pl.BlockSpec((D_A + D_B, D_MODEL), lambda i: (0, 0)),
        ],
        out_specs=pl.BlockSpec((tm, D_MODEL), row),
        out_shape=jax.ShapeDtypeStruct((m, D_MODEL), jnp.float32),
        name="out_projection",
        compiler_params=pltpu.CompilerParams(
            dimension_semantics=("arbitrary",),
            vmem_limit_bytes=VMEM_LIMIT),
    )(x2d, ya, yb, w_out_bf16)


def _rope_tables(rows, cols):
    half = HEAD_DIM // 2
    inv_freq = ROPE_THETA ** (-jnp.arange(0, half, 2, dtype=jnp.float32) / half)
    ang_r = rows.astype(jnp.float32)[:, None] * inv_freq[None, :]
    ang_c = cols.astype(jnp.float32)[:, None] * inv_freq[None, :]
    n_r, n_c = rows.shape[0], cols.shape[0]

    def grid(r_part, c_part):
        r_part = jnp.broadcast_to(r_part[:, None, :], (n_r, n_c, r_part.shape[-1]))
        c_part = jnp.broadcast_to(c_part[None, :, :], (n_r, n_c, c_part.shape[-1]))
        return jnp.concatenate([r_part, c_part], axis=-1).reshape(n_r * n_c, HEAD_DIM)

    cr, sr, cc, sc = jnp.cos(ang_r), jnp.sin(ang_r), jnp.cos(ang_c), jnp.sin(ang_c)
    zr, zc = jnp.zeros_like(sr), jnp.zeros_like(sc)
    cos = grid(jnp.concatenate([cr, cr], -1), jnp.concatenate([cc, cc], -1))
    sin_a = grid(jnp.concatenate([-sr, zr], -1), jnp.concatenate([-sc, zc], -1))
    sin_b = grid(jnp.concatenate([zr, sr], -1), jnp.concatenate([zc, sc], -1))
    return cos, sin_a, sin_b


def _nbr_bias_tables(rpb):
    rows = 32
    tile_r = np.array([0, 2, 8, rows - 4, rows - 2])[:, None, None, None, None]
    dr = np.arange(TILES_PER_QUERY_ROWS)[None, :, None, None, None]
    c = np.arange(GRID_W)[None, None, :, None, None]
    jj = np.arange(NB_KROWS)[None, None, None, :, None]
    kc = np.arange(GRID_W)[None, None, None, None, :]
    k0 = np.clip(tile_r - WIN_R // 2, 0, rows - NB_KROWS)
    r = tile_r + dr
    kr = k0 + jj
    r0 = np.clip(r - WIN_R // 2, 0, rows - WIN_R)
    c0 = np.clip(c - WIN_C // 2, 0, GRID_W - WIN_C)
    valid = (kr >= r0) & (kr < r0 + WIN_R) & (kc >= c0) & (kc < c0 + WIN_C)
    valid = np.broadcast_to(valid, (NB_CLASSES, TILES_PER_QUERY_ROWS, GRID_W, NB_KROWS, GRID_W))
    valid = valid.reshape(NB_CLASSES, NB_Q, NB_K)
    off_r = np.clip(kr - r + (WIN_R - 1), 0, 2 * WIN_R - 2).reshape(-1)
    n_off_c = 2 * WIN_C - 1
    pos = np.arange(GRID_W)
    onehot = (pos[None, None, :] - pos[None, :, None] + (WIN_C - 1)
              == np.arange(n_off_c)[:, None, None]).astype(np.float32)
    toeplitz = jnp.einsum('hro,ock->hrck', rpb.astype(jnp.float32) * LOG2E, onehot,
                          precision=lax.Precision.HIGHEST)
    tab = jnp.take(toeplitz, off_r, axis=1)
    tab = tab.reshape(N_HEADS_B, NB_CLASSES, TILES_PER_QUERY_ROWS, NB_KROWS, GRID_W, GRID_W)
    tab = tab.transpose(0, 1, 2, 4, 3, 5).reshape(N_HEADS_B, NB_CLASSES, NB_Q, NB_K)
    return jnp.where(valid[None], tab, MASK_VALUE)


def kernel(x_prompt, x_sample, meta_tokens, norm_w, w_in, q_norm_a, k_norm_a, q_norm_b, k_norm_b, rpb, w_out):
    w_in_bf = w_in[0].astype(jnp.bfloat16)
    w_out_bf = w_out[0].astype(jnp.bfloat16)
    nw = norm_w[0].reshape(1, D_MODEL)
    gains = [g[0].reshape(1, HEAD_DIM) for g in (q_norm_a, k_norm_a, q_norm_b, k_norm_b)]
    bias = _nbr_bias_tables(rpb[0])

    meta_tabs = _rope_tables(jnp.full((1,), -1, jnp.int32), jnp.arange(META_PAD, dtype=jnp.int32))
    meta_x = jnp.pad(meta_tokens, ((0, META_PAD - N_META), (0, 0)))
    score_bound_a = (HEAD_DIM * Q_SCALE * jnp.max(jnp.abs(q_norm_a[0])) * jnp.max(jnp.abs(k_norm_a[0])))
    safe_a = (score_bound_a <= SAFE_SCORE_BOUND).astype(jnp.int32).reshape(1)
    score_bound_b = (HEAD_DIM * Q_SCALE * jnp.max(jnp.abs(q_norm_b[0])) * jnp.max(jnp.abs(k_norm_b[0]))
                     + LOG2E * jnp.max(jnp.abs(rpb[0])))
    safe_b = (score_bound_b <= SAFE_SCORE_BOUND).astype(jnp.int32).reshape(1)
    meta_out = _in_projection(meta_x, META_PAD, META_PAD, META_PAD, nw, w_in_bf, gains, meta_tabs)
    _, km_a, vmt_a, _, _, km_b, vm_b, _ = meta_out

    def encode(x):
        batch, seq, _ = x.shape
        x2d = x.reshape(batch * seq, D_MODEL)
        tabs = _rope_tables(jnp.arange(seq // GRID_W, dtype=jnp.int32),
                            jnp.arange(GRID_W, dtype=jnp.int32))
        qat, ka, vat, ga, qb, kb, vb, gb = _in_projection(
            x2d, seq, IN_ROW_TILE, GQA_KEY_CHUNK, nw, w_in_bf, gains, tabs)
        ya = _global_attention(safe_a, qat, ka, vat, ga, km_a, vmt_a, batch, seq, GQA_Q_TILE)
        yb = _neighbourhood_attention(safe_b, qb, kb, vb, gb, km_b, vm_b, bias, batch, seq,
                                      NB_ROWS_PER_STEP)
        y = _out_projection(x2d, ya, yb, w_out_bf, OUT_ROW_TILE)
        return y.reshape(batch, seq, D_MODEL)

    return (encode(x_prompt), encode(x_sample))
```

```python
import functools
import math

import jax
import jax.numpy as jnp
import numpy as np
from jax import lax
from jax.experimental import pallas as pl
from jax.experimental.pallas import tpu as pltpu

D_MODEL = 2048
HEAD_DIM = 128
N_HEADS_A = 8
N_KV_A = 2
GROUP_A = N_HEADS_A // N_KV_A
N_HEADS_B = 8
D_A = N_HEADS_A * HEAD_DIM
D_KV_A = N_KV_A * HEAD_DIM
D_B = N_HEADS_B * HEAD_DIM
D_IN = D_A + 2 * D_KV_A + D_A + 4 * D_B
N_META = 16
GRID_W = 64
WIN_R = 8
WIN_C = 16
ROPE_THETA = 10000.0
EPS = 1e-6

LOG2E = math.log2(math.e)
Q_SCALE = HEAD_DIM ** -0.5 * LOG2E
MASK_VALUE = -1e30

LANES = 128
BF16_SUBLANE_TILE = 16
VT_EXTRA_ROWS = BF16_SUBLANE_TILE
VT_ROWS = HEAD_DIM + VT_EXTRA_ROWS
SAFE_SCORE_BOUND = 60.0
META_PAD = LANES
VMEM_LIMIT = 56 * 1024 * 1024

TN = 512
N_COL_TILES = D_IN // TN
TILES_PER_QUERY_ROWS = 2
NB_Q = TILES_PER_QUERY_ROWS * GRID_W
NB_KROWS = WIN_R + TILES_PER_QUERY_ROWS - 1
NB_K = NB_KROWS * GRID_W
NB_CLASSES = 5

IN_ROW_TILE = 256
OUT_ROW_TILE = 512
GQA_KEY_CHUNK = 512
GQA_Q_TILE = 512
NB_ROWS_PER_STEP = 1024
NB_INTERLEAVE = 8


def _silu(z):
    return z * (1.0 / (1.0 + jnp.exp(-z)))


def _head_norm(a, gain):
    ms = jnp.mean(a * a, axis=-1, keepdims=True)
    return a * lax.rsqrt(ms + EPS) * gain


def _rope(y, cos, sin_a, sin_b):
    return y * cos + pltpu.roll(y, 96, 1) * sin_a + pltpu.roll(y, 32, 1) * sin_b


def _inproj_kernel(x_ref, nw_ref, w_ref, gqa_ref, gka_ref, gqb_ref, gkb_ref,
                   cos_ref, sa_ref, sb_ref,
                   qat_ref, ka_ref, vat_ref, za_ref, qb_ref, kb_ref, vb_ref, zb_ref,
                   xn_ref):
    x = x_ref[...]
    ms = jnp.mean(x * x, axis=-1, keepdims=True)
    xn_ref[...] = (x * lax.rsqrt(ms + EPS) * nw_ref[...]).astype(jnp.bfloat16)

    def column_tile(j):
        return jnp.dot(xn_ref[...], w_ref[:, j * TN:(j + 1) * TN],
                       preferred_element_type=jnp.float32)

    def normed_roped(a, gain_ref):
        return _rope(_head_norm(a, gain_ref[...]), cos_ref[...], sa_ref[...], sb_ref[...])

    def store_heads(ref, first, vals):
        for h, v in enumerate(vals):
            ref[:, (first + h) * HEAD_DIM:(first + h + 1) * HEAD_DIM] = v.astype(ref.dtype)

    heads_per_tile = TN // HEAD_DIM

    def epilogue(j, acc):
        heads = [acc[:, h * HEAD_DIM:(h + 1) * HEAD_DIM] for h in range(heads_per_tile)]
        if j < 2:
            for h, a in enumerate(heads):
                q = normed_roped(a, gqa_ref) * Q_SCALE
                qat_ref[j * heads_per_tile + h] = q.T.astype(qat_ref.dtype)
        elif j == 2:
            store_heads(ka_ref, 0, [normed_roped(a, gka_ref) for a in heads[:N_KV_A]])
            ones_row = (lax.broadcasted_iota(jnp.int32, (VT_EXTRA_ROWS, acc.shape[0]), 0) == 0)
            for g, a in enumerate(heads[N_KV_A:]):
                vat_ref[g, 0, :HEAD_DIM, :] = a.T.astype(vat_ref.dtype)
                vat_ref[g, 0, HEAD_DIM:, :] = ones_row.astype(vat_ref.dtype)
        elif j < 5:
            za_ref[:, (j - 3) * TN:(j - 2) * TN] = _silu(acc)
        elif j < 7:
            store_heads(qb_ref, (j - 5) * heads_per_tile,
                        [_head_norm(a, gqb_ref[...]) * Q_SCALE for a in heads])
        elif j < 9:
            store_heads(kb_ref, (j - 7) * heads_per_tile,
                        [_head_norm(a, gkb_ref[...]) for a in heads])
        elif j < 11:
            vb_ref[:, (j - 9) * TN:(j - 8) * TN] = acc.astype(vb_ref.dtype)
        else:
            zb_ref[:, (j - 11) * TN:(j - 10) * TN] = _silu(acc)

    acc = column_tile(0)
    for j in range(N_COL_TILES):
        nxt = column_tile(j + 1) if j + 1 < N_COL_TILES else None
        epilogue(j, acc)
        acc = nxt


def _in_projection(x2d, seq, tm, chunk, norm_w, w_in_bf16, gains, tabs):
    m = x2d.shape[0]
    n_tab = seq // tm
    sub = chunk // tm
    cos, sin_a, sin_b = tabs
    row = lambda i: (i, 0)
    const = lambda i: (0, 0)
    bf, f32 = jnp.bfloat16, jnp.float32
    out_shape = [
        jax.ShapeDtypeStruct((N_HEADS_A, HEAD_DIM, m), bf), jax.ShapeDtypeStruct((m, D_KV_A), bf),
        jax.ShapeDtypeStruct((N_KV_A, m // chunk, VT_ROWS, chunk), bf),
        jax.ShapeDtypeStruct((m, D_A), f32),
        jax.ShapeDtypeStruct((m, D_B), bf), jax.ShapeDtypeStruct((m, D_B), bf),
        jax.ShapeDtypeStruct((m, D_B), bf), jax.ShapeDtypeStruct((m, D_B), f32),
    ]
    out_specs = [
        pl.BlockSpec((N_HEADS_A, HEAD_DIM, tm), lambda i: (0, 0, i)),
        pl.BlockSpec((tm, D_KV_A), row),
        pl.BlockSpec((N_KV_A, 1, VT_ROWS, tm), lambda i: (0, i // sub, 0, i % sub)),
        pl.BlockSpec((tm, D_A), row),
        pl.BlockSpec((tm, D_B), row), pl.BlockSpec((tm, D_B), row),
        pl.BlockSpec((tm, D_B), row), pl.BlockSpec((tm, D_B), row),
    ]
    gain_spec = pl.BlockSpec((1, HEAD_DIM), const)
    tab_spec = pl.BlockSpec((tm, HEAD_DIM), lambda i: (i % n_tab, 0))
    return pl.pallas_call(
        _inproj_kernel,
        grid=(m // tm,),
        in_specs=[
            pl.BlockSpec((tm, D_MODEL), row),
            pl.BlockSpec((1, D_MODEL), const),
            pl.BlockSpec((D_MODEL, D_IN), const, pipeline_mode=pl.Buffered(1)),
            gain_spec, gain_spec, gain_spec, gain_spec,
            tab_spec, tab_spec, tab_spec,
        ],
        out_specs=out_specs,
        out_shape=out_shape,
        scratch_shapes=[pltpu.VMEM((tm, D_MODEL), bf)],
        name="in_projection",
        compiler_params=pltpu.CompilerParams(
            dimension_semantics=("arbitrary",),
            vmem_limit_bytes=VMEM_LIMIT),
    )(x2d, norm_w, w_in_bf16, *gains, cos, sin_a, sin_b)


def _nt_dot(a, b):
    return lax.dot_general(a, b, (((1,), (1,)), ((), ())), preferred_element_type=jnp.float32)


GQA_SLAB_WIDTH = 512


def _gqa_kernel(safe_ref, qt_ref, k_ref, vt_ref, km_ref, vmt_ref, g_ref, o_ref, acc_ref, s_ref):
    tq = qt_ref.shape[2]
    n_chunks, chunk = vt_ref.shape[1], vt_ref.shape[3]
    width = acc_ref.shape[2]
    n_slabs = acc_ref.shape[0]
    heads_per_slab = width // tq
    row = lax.broadcasted_iota(jnp.int32, (META_PAD, width), 0)
    meta_bias = jnp.where(row < N_META, 0.0, MASK_VALUE).astype(jnp.float32)
    km = km_ref[...]
    vmt = vmt_ref[0, 0]
    qts = [jnp.concatenate([qt_ref[i * heads_per_slab + j] for j in range(heads_per_slab)], axis=1)
           if heads_per_slab > 1 else qt_ref[i] for i in range(n_slabs)]

    def dot(a, b):
        return jnp.dot(a, b, preferred_element_type=jnp.float32)

    def keys(c):
        return k_ref[pl.ds(pl.multiple_of(c * chunk, chunk), chunk), :]

    def attend(init, consume):
        s_ref[0] = dot(keys(0), qts[0])
        s_meta = [dot(km, qt) + meta_bias for qt in qts]
        state = tuple(init(i, s) for i, s in enumerate(s_meta))

        def body(c, state):
            state = list(state)
            k = keys(c)
            k_next = keys(jnp.minimum(c + 1, n_chunks - 1))
            vt = vt_ref[0, c]
            for i in range(n_slabs):
                nxt = dot(k, qts[i + 1]) if i + 1 < n_slabs else dot(k_next, qts[0])
                s_ref[(i + 1) % 2] = nxt
                state[i] = consume(i, s_ref[i % 2], vt, state[i])
            return tuple(state)

        lax.fori_loop(0, n_chunks, body, state)
        for i in range(n_slabs):
            acc = acc_ref[i]
            o_slab = acc[:HEAD_DIM] * (1.0 / acc[HEAD_DIM:HEAD_DIM + 1])
            for j in range(heads_per_slab):
                h = i * heads_per_slab + j
                sl = slice(h * HEAD_DIM, (h + 1) * HEAD_DIM)
                o = o_slab[:, j * tq:(j + 1) * tq].T
                o_ref[:, sl] = (o * g_ref[:, sl]).astype(o_ref.dtype)

    safe = safe_ref[0] != 0

    @pl.when(safe)
    def _():
        def init(i, s):
            acc_ref[i] = dot(vmt, jnp.exp2(s).astype(jnp.bfloat16))
            return 0

        def consume(i, s, vt, state):
            acc_ref[i] += dot(vt, jnp.exp2(s).astype(jnp.bfloat16))
            return state

        attend(init, consume)

    @pl.when(jnp.logical_not(safe))
    def _():
        def init(i, s):
            m = jnp.max(s, axis=0, keepdims=True)
            acc_ref[i] = dot(vmt, jnp.exp2(s - m).astype(jnp.bfloat16))
            return m

        def consume(i, s, vt, m):
            m_new = jnp.maximum(m, jnp.max(s, axis=0, keepdims=True))
            p = jnp.exp2(s - m_new)
            acc_ref[i] = jnp.exp2(m - m_new) * acc_ref[i] + dot(vt, p.astype(jnp.bfloat16))
            return m_new

        attend(init, consume)


def _global_attention(safe, qat, ka, vat, gate, km, vmt, batch, seq, tq):
    m = gate.shape[0]
    nq = seq // tq
    chunk = vat.shape[3]
    n_chunks = seq // chunk
    qmap = lambda b, g, i: (b * nq + i, g)
    width = GROUP_A * HEAD_DIM
    return pl.pallas_call(
        _gqa_kernel,
        grid=(batch, N_KV_A, nq),
        in_specs=[
            pl.BlockSpec(memory_space=pltpu.SMEM),
            pl.BlockSpec((GROUP_A, HEAD_DIM, tq), lambda b, g, i: (g, 0, b * nq + i)),
            pl.BlockSpec((seq, HEAD_DIM), lambda b, g, i: (b, g)),
            pl.BlockSpec((1, n_chunks, VT_ROWS, chunk), lambda b, g, i: (g, b, 0, 0)),
            pl.BlockSpec((META_PAD, HEAD_DIM), lambda b, g, i: (0, g)),
            pl.BlockSpec((1, 1, VT_ROWS, META_PAD), lambda b, g, i: (g, 0, 0, 0)),
            pl.BlockSpec((tq, width), qmap),
        ],
        out_specs=pl.BlockSpec((tq, width), qmap),
        out_shape=jax.ShapeDtypeStruct((m, D_A), jnp.bfloat16),
        scratch_shapes=[pltpu.VMEM((GROUP_A * tq // GQA_SLAB_WIDTH, VT_ROWS, GQA_SLAB_WIDTH), jnp.float32),
                        pltpu.VMEM((2, chunk, GQA_SLAB_WIDTH), jnp.float32)],
        name="global_attention",
        compiler_params=pltpu.CompilerParams(
            dimension_semantics=("arbitrary", "arbitrary", "arbitrary"),
            vmem_limit_bytes=VMEM_LIMIT),
    )(safe, qat, ka, vat, km, vmt, gate)


def _nbr_kernel(safe_ref, q_ref, k_ref, v_ref, km_ref, vm_ref, bias_ref, g_ref, o_ref, *, n_tiles):
    tiles_per_step = q_ref.shape[0] // NB_Q
    n_rows = n_tiles * TILES_PER_QUERY_ROWS
    lane = lax.broadcasted_iota(jnp.int32, (NB_Q, META_PAD), 1)
    meta_bias = jnp.where(lane < N_META, 0.0, MASK_VALUE).astype(jnp.float32)
    km = km_ref[...]
    vm = vm_ref[...]
    step = pl.program_id(2)
    bf = jnp.bfloat16

    def dot(a, b):
        return jnp.dot(a, b, preferred_element_type=jnp.float32)

    def ones_column(n):
        return (lax.broadcasted_iota(jnp.int32, (n, LANES), 1) == 0).astype(bf)

    def make_body(probs_fn, with_ones):
        def body(i, _):
            tiles = []
            for u in range(NB_INTERLEAVE):
                local = i * NB_INTERLEAVE + u
                t = step * tiles_per_step + local
                cls = jnp.where(t < 2, t, jnp.where(t >= n_tiles - 2, t - n_tiles + NB_CLASSES, 2))
                k0 = jnp.clip(t * TILES_PER_QUERY_ROWS - WIN_R // 2, 0, n_rows - NB_KROWS)
                kstart = pl.multiple_of(k0 * GRID_W, GRID_W)
                qrows = pl.ds(pl.multiple_of(local * NB_Q, NB_Q), NB_Q)
                tiles.append((cls, kstart, qrows))
            scores = []
            for cls, kstart, qrows in tiles:
                q = q_ref[qrows, :]
                s_meta = _nt_dot(q, km) + meta_bias
                s_win = _nt_dot(q, k_ref[pl.ds(kstart, NB_K), :]) + bias_ref[0, cls]
                scores.append((s_meta, s_win))
            probs = [probs_fn(s_meta, s_win) for s_meta, s_win in scores]
            for (cls, kstart, qrows), (p_meta, p_win, l) in zip(tiles, probs):
                v = v_ref[pl.ds(kstart, NB_K), :]
                v_meta = vm
                if with_ones:
                    v = jnp.concatenate([v, ones_column(NB_K)], axis=1)
                    v_meta = jnp.concatenate([vm, ones_column(META_PAD)], axis=1)
                acc = dot(p_meta, v_meta) + dot(p_win, v)
                if with_ones:
                    l = acc[:, HEAD_DIM:HEAD_DIM + 1]
                    acc = acc[:, :HEAD_DIM]
                o_ref[qrows, :] = (acc * (1.0 / l) * g_ref[qrows, :]).astype(o_ref.dtype)
            return 0
        return body

    def probs_safe(s_meta, s_win):
        return jnp.exp2(s_meta).astype(bf), jnp.exp2(s_win).astype(bf), None

    def probs_online_max(s_meta, s_win):
        m = jnp.maximum(jnp.max(s_meta, axis=-1, keepdims=True),
                        jnp.max(s_win, axis=-1, keepdims=True))
        p_meta = jnp.exp2(s_meta - m)
        p_win = jnp.exp2(s_win - m)
        l = jnp.sum(p_meta, axis=-1, keepdims=True) + jnp.sum(p_win, axis=-1, keepdims=True)
        return p_meta.astype(bf), p_win.astype(bf), l

    safe = safe_ref[0] != 0
    n_trips = tiles_per_step // NB_INTERLEAVE

    @pl.when(safe)
    def _():
        lax.fori_loop(0, n_trips, make_body(probs_safe, True), 0)

    @pl.when(jnp.logical_not(safe))
    def _():
        lax.fori_loop(0, n_trips, make_body(probs_online_max, False), 0)


def _neighbourhood_attention(safe, qb, kb, vb, gate, km, vm, bias, batch, seq, tqb):
    m = qb.shape[0]
    nq = seq // tqb
    qmap = lambda b, h, i: (b * nq + i, h)
    kvmap = lambda b, h, i: (b, h)
    meta_map = lambda b, h, i: (0, h)
    return pl.pallas_call(
        functools.partial(_nbr_kernel, n_tiles=seq // NB_Q),
        grid=(batch, N_HEADS_B, nq),
        in_specs=[
            pl.BlockSpec(memory_space=pltpu.SMEM),
            pl.BlockSpec((tqb, HEAD_DIM), qmap),
            pl.BlockSpec((seq, HEAD_DIM), kvmap),
            pl.BlockSpec((seq, HEAD_DIM), kvmap),
            pl.BlockSpec((META_PAD, HEAD_DIM), meta_map),
            pl.BlockSpec((META_PAD, HEAD_DIM), meta_map),
            pl.BlockSpec((1, NB_CLASSES, NB_Q, NB_K), lambda b, h, i: (h, 0, 0, 0)),
            pl.BlockSpec((tqb, HEAD_DIM), qmap),
        ],
        out_specs=pl.BlockSpec((tqb, HEAD_DIM), qmap),
        out_shape=jax.ShapeDtypeStruct((m, D_B), jnp.bfloat16),
        name="neighbourhood_attention",
        compiler_params=pltpu.CompilerParams(
            dimension_semantics=("arbitrary", "arbitrary", "arbitrary"),
            vmem_limit_bytes=VMEM_LIMIT),
    )(safe, qb, kb, vb, km, vm, bias, gate)


def _outproj_kernel(x_ref, ya_ref, yb_ref, w_ref, o_ref):
    y = jnp.dot(ya_ref[...], w_ref[:D_A, :], preferred_element_type=jnp.float32)
    y = y + jnp.dot(yb_ref[...], w_ref[D_A:, :], preferred_element_type=jnp.float32)
    o_ref[...] = x_ref[...] + y


def _out_projection(x2d, ya, yb, w_out_bf16, tm):
    m = x2d.shape[0]
    row = lambda i: (i, 0)
    return pl.pallas_call(
        _outproj_kernel,
        grid=(m // tm,),
        in_specs=[
            pl.BlockSpec((tm, D_MODEL), row),
            pl.BlockSpec((tm, D_A), row),
            pl.BlockSpec((tm, D_B), row),
            pl.BlockSpec((D_A + D_B, D_MODEL), lambda i: (0, 0)),
        ],
        out_specs=pl.BlockSpec((tm, D_MODEL), row),
        out_shape=jax.ShapeDtypeStruct((m, D_MODEL), jnp.float32),
        name="out_projection",
        compiler_params=pltpu.CompilerParams(
            dimension_semantics=("arbitrary",),
            vmem_limit_bytes=VMEM_LIMIT),
    )(x2d, ya, yb, w_out_bf16)


def _rope_tables(rows, cols):
    half = HEAD_DIM // 2
    inv_freq = ROPE_THETA ** (-jnp.arange(0, half, 2, dtype=jnp.float32) / half)
    ang_r = rows.astype(jnp.float32)[:, None] * inv_freq[None, :]
    ang_c = cols.astype(jnp.float32)[:, None] * inv_freq[None, :]
    n_r, n_c = rows.shape[0], cols.shape[0]

    def grid(r_part, c_part):
        r_part = jnp.broadcast_to(r_part[:, None, :], (n_r, n_c, r_part.shape[-1]))
        c_part = jnp.broadcast_to(c_part[None, :, :], (n_r, n_c, c_part.shape[-1]))
        return jnp.concatenate([r_part, c_part], axis=-1).reshape(n_r * n_c, HEAD_DIM)

    cr, sr, cc, sc = jnp.cos(ang_r), jnp.sin(ang_r), jnp.cos(ang_c), jnp.sin(ang_c)
    zr, zc = jnp.zeros_like(sr), jnp.zeros_like(sc)
    cos = grid(jnp.concatenate([cr, cr], -1), jnp.concatenate([cc, cc], -1))
    sin_a = grid(jnp.concatenate([-sr, zr], -1), jnp.concatenate([-sc, zc], -1))
    sin_b = grid(jnp.concatenate([zr, sr], -1), jnp.concatenate([zc, sc], -1))
    return cos, sin_a, sin_b


def _nbr_bias_tables(rpb):
    rows = 32
    tile_r = np.array([0, 2, 8, rows - 4, rows - 2])[:, None, None]
    dr = np.arange(TILES_PER_QUERY_ROWS)[None, :, None]
    jj = np.arange(NB_KROWS)[None, None, :]
    k0 = np.clip(tile_r - WIN_R // 2, 0, rows - NB_KROWS)
    r = tile_r + dr
    kr = k0 + jj
    r0 = np.clip(r - WIN_R // 2, 0, rows - WIN_R)
    row_valid = (kr >= r0) & (kr < r0 + WIN_R)
    off_r = np.clip(kr - r + (WIN_R - 1), 0, 2 * WIN_R - 2)
    n_off_c = 2 * WIN_C - 1
    pos = np.arange(GRID_W)
    onehot = (pos[None, None, :] - pos[None, :, None] + (WIN_C - 1)
              == np.arange(n_off_c)[:, None, None]).astype(np.float32)
    toeplitz = jnp.einsum('hro,ock->hrck', rpb.astype(jnp.float32) * LOG2E, onehot,
                          precision=lax.Precision.HIGHEST)

    def assemble(toep_ref, out_ref):
        c = lax.broadcasted_iota(jnp.int32, (GRID_W, GRID_W), 0)
        kc = lax.broadcasted_iota(jnp.int32, (GRID_W, GRID_W), 1)
        c0 = jnp.clip(c - WIN_C // 2, 0, GRID_W - WIN_C)
        col_valid = (kc >= c0) & (kc < c0 + WIN_C)
        masked = jnp.full((GRID_W, GRID_W), MASK_VALUE, jnp.float32)
        for cls in range(NB_CLASSES):
            for d in range(TILES_PER_QUERY_ROWS):
                blocks = [jnp.where(col_valid, toep_ref[0, int(off_r[cls, d, j])], MASK_VALUE)
                          if row_valid[cls, d, j] else masked for j in range(NB_KROWS)]
                out_ref[0, cls, d * GRID_W:(d + 1) * GRID_W, :] = jnp.concatenate(blocks, axis=1)

    return pl.pallas_call(
        assemble,
        grid=(N_HEADS_B,),
        in_specs=[pl.BlockSpec((1, 2 * WIN_R - 1, GRID_W, GRID_W), lambda h: (h, 0, 0, 0))],
        out_specs=pl.BlockSpec((1, NB_CLASSES, NB_Q, NB_K), lambda h: (h, 0, 0, 0)),
        out_shape=jax.ShapeDtypeStruct((N_HEADS_B, NB_CLASSES, NB_Q, NB_K), jnp.float32),
        name="nbr_bias_table",
    )(toeplitz)


def kernel(x_prompt, x_sample, meta_tokens, norm_w, w_in, q_norm_a, k_norm_a, q_norm_b, k_norm_b, rpb, w_out):
    w_in_bf = w_in[0].astype(jnp.bfloat16)
    w_out_bf = w_out[0].astype(jnp.bfloat16)
    nw = norm_w[0].reshape(1, D_MODEL)
    gains = [g[0].reshape(1, HEAD_DIM) for g in (q_norm_a, k_norm_a, q_norm_b, k_norm_b)]
    bias = _nbr_bias_tables(rpb[0])

    meta_tabs = _rope_tables(jnp.full((1,), -1, jnp.int32), jnp.arange(META_PAD, dtype=jnp.int32))
    meta_x = jnp.pad(meta_tokens, ((0, META_PAD - N_META), (0, 0)))
    score_bound_a = (HEAD_DIM * Q_SCALE * jnp.max(jnp.abs(q_norm_a[0])) * jnp.max(jnp.abs(k_norm_a[0])))
    safe_a = (score_bound_a <= SAFE_SCORE_BOUND).astype(jnp.int32).reshape(1)
    score_bound_b = (HEAD_DIM * Q_SCALE * jnp.max(jnp.abs(q_norm_b[0])) * jnp.max(jnp.abs(k_norm_b[0]))
                     + LOG2E * jnp.max(jnp.abs(rpb[0])))
    safe_b = (score_bound_b <= SAFE_SCORE_BOUND).astype(jnp.int32).reshape(1)
    meta_out = _in_projection(meta_x, META_PAD, META_PAD, META_PAD, nw, w_in_bf, gains, meta_tabs)
    _, km_a, vmt_a, _, _, km_b, vm_b, _ = meta_out

    def encode(x):
        batch, seq, _ = x.shape
        x2d = x.reshape(batch * seq, D_MODEL)
        tabs = _rope_tables(jnp.arange(seq // GRID_W, dtype=jnp.int32),
                            jnp.arange(GRID_W, dtype=jnp.int32))
        qat, ka, vat, ga, qb, kb, vb, gb = _in_projection(
            x2d, seq, IN_ROW_TILE, GQA_KEY_CHUNK, nw, w_in_bf, gains, tabs)
        ya = _global_attention(safe_a, qat, ka, vat, ga, km_a, vmt_a, batch, seq, GQA_Q_TILE)
        yb = _neighbourhood_attention(safe_b, qb, kb, vb, gb, km_b, vm_b, bias, batch, seq,
                                      NB_ROWS_PER_STEP)
        y = _out_projection(x2d, ya, yb, w_out_bf, OUT_ROW_TILE)
        return y.reshape(batch, seq, D_MODEL)

    return (encode(x_prompt), encode(x_sample))
```

```python
import functools
import math

import jax
import jax.numpy as jnp
import numpy as np
from jax import lax
from jax.experimental import pallas as pl
from jax.experimental.pallas import tpu as pltpu

D_MODEL = 2048
HEAD_DIM = 128
N_HEADS_A = 8
N_KV_A = 2
GROUP_A = N_HEADS_A // N_KV_A
N_HEADS_B = 8
D_A = N_HEADS_A * HEAD_DIM
D_KV_A = N_KV_A * HEAD_DIM
D_B = N_HEADS_B * HEAD_DIM
D_IN = D_A + 2 * D_KV_A + D_A + 4 * D_B
N_META = 16
GRID_W = 64
WIN_R = 8
WIN_C = 16
ROPE_THETA = 10000.0
EPS = 1e-6

LOG2E = math.log2(math.e)
Q_SCALE = HEAD_DIM ** -0.5 * LOG2E
MASK_VALUE = -1e30

LANES = 128
BF16_SUBLANE_TILE = 16
VT_EXTRA_ROWS = BF16_SUBLANE_TILE
VT_ROWS = HEAD_DIM + VT_EXTRA_ROWS
SAFE_SCORE_BOUND = 60.0
META_PAD = LANES
VMEM_LIMIT = 56 * 1024 * 1024

TN = 512
N_COL_TILES = D_IN // TN
NB_QROWS = 4
NB_Q = NB_QROWS * GRID_W
NB_KROWS = 3 * NB_QROWS
NB_K = NB_KROWS * GRID_W
NB_CLASSES = 3

IN_ROW_TILE = 256
OUT_ROW_TILE = 512
GQA_KEY_CHUNK = 512
GQA_Q_TILE = 512
NB_TILES_PER_STEP = 4
NB_VCHUNK = IN_ROW_TILE


def _silu(z):
    return z * (1.0 / (1.0 + jnp.exp(-z)))


def _head_norm(a, gain):
    ms = jnp.mean(a * a, axis=-1, keepdims=True)
    return a * lax.rsqrt(ms + EPS) * gain


def _rope(y, cos, sin_a, sin_b):
    return y * cos + pltpu.roll(y, 96, 1) * sin_a + pltpu.roll(y, 32, 1) * sin_b


def _inproj_kernel(x_ref, nw_ref, w_ref, gqa_ref, gka_ref, gqb_ref, gkb_ref,
                   cos_ref, sa_ref, sb_ref,
                   qat_ref, ka_ref, vat_ref, za_ref, qbt_ref, kb_ref, vbt_ref, zb_ref,
                   xn_ref):
    x = x_ref[...]
    ms = jnp.mean(x * x, axis=-1, keepdims=True)
    xn_ref[...] = (x * lax.rsqrt(ms + EPS) * nw_ref[...]).astype(jnp.bfloat16)

    def column_tile(j):
        return jnp.dot(xn_ref[...], w_ref[:, j * TN:(j + 1) * TN],
                       preferred_element_type=jnp.float32)

    def normed_roped(a, gain_ref):
        return _rope(_head_norm(a, gain_ref[...]), cos_ref[...], sa_ref[...], sb_ref[...])

    def store_heads(ref, first, vals):
        for h, v in enumerate(vals):
            ref[:, (first + h) * HEAD_DIM:(first + h + 1) * HEAD_DIM] = v.astype(ref.dtype)

    heads_per_tile = TN // HEAD_DIM

    def store_values_t(ref, first, vals):
        ones_row = (lax.broadcasted_iota(jnp.int32, (VT_EXTRA_ROWS, vals[0].shape[0]), 0) == 0)
        for h, a in enumerate(vals):
            ref[first + h, 0, :HEAD_DIM, :] = a.T.astype(ref.dtype)
            ref[first + h, 0, HEAD_DIM:, :] = ones_row.astype(ref.dtype)

    def epilogue(j, acc):
        heads = [acc[:, h * HEAD_DIM:(h + 1) * HEAD_DIM] for h in range(heads_per_tile)]
        if j < 2:
            for h, a in enumerate(heads):
                q = normed_roped(a, gqa_ref) * Q_SCALE
                qat_ref[j * heads_per_tile + h] = q.T.astype(qat_ref.dtype)
        elif j == 2:
            store_heads(ka_ref, 0, [normed_roped(a, gka_ref) for a in heads[:N_KV_A]])
            store_values_t(vat_ref, 0, heads[N_KV_A:])
        elif j < 5:
            za_ref[:, (j - 3) * TN:(j - 2) * TN] = _silu(acc)
        elif j < 7:
            for h, a in enumerate(heads):
                q = _head_norm(a, gqb_ref[...]) * Q_SCALE
                qbt_ref[(j - 5) * heads_per_tile + h] = q.T.astype(qbt_ref.dtype)
        elif j < 9:
            store_heads(kb_ref, (j - 7) * heads_per_tile,
                        [_head_norm(a, gkb_ref[...]) for a in heads])
        elif j < 11:
            store_values_t(vbt_ref, (j - 9) * heads_per_tile, heads)
        else:
            zb_ref[:, (j - 11) * TN:(j - 10) * TN] = _silu(acc)

    acc = column_tile(0)
    for j in range(N_COL_TILES):
        nxt = column_tile(j + 1) if j + 1 < N_COL_TILES else None
        epilogue(j, acc)
        acc = nxt


def _in_projection(x2d, seq, tm, chunk, norm_w, w_in_bf16, gains, tabs):
    m = x2d.shape[0]
    n_tab = seq // tm
    sub = chunk // tm
    cos, sin_a, sin_b = tabs
    row = lambda i: (i, 0)
    const = lambda i: (0, 0)
    bf, f32 = jnp.bfloat16, jnp.float32
    out_shape = [
        jax.ShapeDtypeStruct((N_HEADS_A, HEAD_DIM, m), bf), jax.ShapeDtypeStruct((m, D_KV_A), bf),
        jax.ShapeDtypeStruct((N_KV_A, m // chunk, VT_ROWS, chunk), bf),
        jax.ShapeDtypeStruct((m, D_A), f32),
        jax.ShapeDtypeStruct((N_HEADS_B, HEAD_DIM, m), bf), jax.ShapeDtypeStruct((m, D_B), bf),
        jax.ShapeDtypeStruct((N_HEADS_B, m // tm, VT_ROWS, tm), bf),
        jax.ShapeDtypeStruct((m, D_B), f32),
    ]
    out_specs = [
        pl.BlockSpec((N_HEADS_A, HEAD_DIM, tm), lambda i: (0, 0, i)),
        pl.BlockSpec((tm, D_KV_A), row),
        pl.BlockSpec((N_KV_A, 1, VT_ROWS, tm), lambda i: (0, i // sub, 0, i % sub)),
        pl.BlockSpec((tm, D_A), row),
        pl.BlockSpec((N_HEADS_B, HEAD_DIM, tm), lambda i: (0, 0, i)),
        pl.BlockSpec((tm, D_B), row),
        pl.BlockSpec((N_HEADS_B, 1, VT_ROWS, tm), lambda i: (0, i, 0, 0)),
        pl.BlockSpec((tm, D_B), row),
    ]
    gain_spec = pl.BlockSpec((1, HEAD_DIM), const)
    tab_spec = pl.BlockSpec((tm, HEAD_DIM), lambda i: (i % n_tab, 0))
    return pl.pallas_call(
        _inproj_kernel,
        grid=(m // tm,),
        in_specs=[
            pl.BlockSpec((tm, D_MODEL), row),
            pl.BlockSpec((1, D_MODEL), const),
            pl.BlockSpec((D_MODEL, D_IN), const, pipeline_mode=pl.Buffered(1)),
            gain_spec, gain_spec, gain_spec, gain_spec,
            tab_spec, tab_spec, tab_spec,
        ],
        out_specs=out_specs,
        out_shape=out_shape,
        scratch_shapes=[pltpu.VMEM((tm, D_MODEL), bf)],
        name="in_projection",
        compiler_params=pltpu.CompilerParams(
            dimension_semantics=("arbitrary",),
            vmem_limit_bytes=VMEM_LIMIT),
    )(x2d, norm_w, w_in_bf16, *gains, cos, sin_a, sin_b)


def _nt_dot(a, b):
    return lax.dot_general(a, b, (((1,), (1,)), ((), ())), preferred_element_type=jnp.float32)


GQA_SLAB_WIDTH = 512


def _gqa_kernel(safe_ref, qt_ref, k_ref, vt_ref, km_ref, vmt_ref, g_ref, o_ref, acc_ref, s_ref):
    tq = qt_ref.shape[2]
    n_chunks, chunk = vt_ref.shape[1], vt_ref.shape[3]
    width = acc_ref.shape[2]
    n_slabs = acc_ref.shape[0]
    heads_per_slab = width // tq
    row = lax.broadcasted_iota(jnp.int32, (META_PAD, width), 0)
    meta_bias = jnp.where(row < N_META, 0.0, MASK_VALUE).astype(jnp.float32)
    km = km_ref[...]
    vmt = vmt_ref[0, 0]
    qts = [jnp.concatenate([qt_ref[i * heads_per_slab + j] for j in range(heads_per_slab)], axis=1)
           if heads_per_slab > 1 else qt_ref[i] for i in range(n_slabs)]

    def dot(a, b):
        return jnp.dot(a, b, preferred_element_type=jnp.float32)

    def keys(c):
        return k_ref[pl.ds(pl.multiple_of(c * chunk, chunk), chunk), :]

    def attend(init, consume):
        s_ref[0] = dot(keys(0), qts[0])
        s_meta = [dot(km, qt) + meta_bias for qt in qts]
        state = tuple(init(i, s) for i, s in enumerate(s_meta))

        def body(c, state):
            state = list(state)
            k = keys(c)
            k_next = keys(jnp.minimum(c + 1, n_chunks - 1))
            vt = vt_ref[0, c]
            for i in range(n_slabs):
                nxt = dot(k, qts[i + 1]) if i + 1 < n_slabs else dot(k_next, qts[0])
                s_ref[(i + 1) % 2] = nxt
                state[i] = consume(i, s_ref[i % 2], vt, state[i])
            return tuple(state)

        lax.fori_loop(0, n_chunks, body, state)
        for i in range(n_slabs):
            acc = acc_ref[i]
            o_slab = acc[:HEAD_DIM] * (1.0 / acc[HEAD_DIM:HEAD_DIM + 1])
            for j in range(heads_per_slab):
                h = i * heads_per_slab + j
                sl = slice(h * HEAD_DIM, (h + 1) * HEAD_DIM)
                o = o_slab[:, j * tq:(j + 1) * tq].T
                o_ref[:, sl] = (o * g_ref[:, sl]).astype(o_ref.dtype)

    safe = safe_ref[0] != 0

    @pl.when(safe)
    def _():
        def init(i, s):
            acc_ref[i] = dot(vmt, jnp.exp2(s).astype(jnp.bfloat16))
            return 0

        def consume(i, s, vt, state):
            acc_ref[i] += dot(vt, jnp.exp2(s).astype(jnp.bfloat16))
            return state

        attend(init, consume)

    @pl.when(jnp.logical_not(safe))
    def _():
        def init(i, s):
            m = jnp.max(s, axis=0, keepdims=True)
            acc_ref[i] = dot(vmt, jnp.exp2(s - m).astype(jnp.bfloat16))
            return m

        def consume(i, s, vt, m):
            m_new = jnp.maximum(m, jnp.max(s, axis=0, keepdims=True))
            p = jnp.exp2(s - m_new)
            acc_ref[i] = jnp.exp2(m - m_new) * acc_ref[i] + dot(vt, p.astype(jnp.bfloat16))
            return m_new

        attend(init, consume)


def _global_attention(safe, qat, ka, vat, gate, km, vmt, batch, seq, tq):
    m = gate.shape[0]
    nq = seq // tq
    chunk = vat.shape[3]
    n_chunks = seq // chunk
    qmap = lambda b, g, i: (b * nq + i, g)
    width = GROUP_A * HEAD_DIM
    return pl.pallas_call(
        _gqa_kernel,
        grid=(batch, N_KV_A, nq),
        in_specs=[
            pl.BlockSpec(memory_space=pltpu.SMEM),
            pl.BlockSpec((GROUP_A, HEAD_DIM, tq), lambda b, g, i: (g, 0, b * nq + i)),
            pl.BlockSpec((seq, HEAD_DIM), lambda b, g, i: (b, g)),
            pl.BlockSpec((1, n_chunks, VT_ROWS, chunk), lambda b, g, i: (g, b, 0, 0)),
            pl.BlockSpec((META_PAD, HEAD_DIM), lambda b, g, i: (0, g)),
            pl.BlockSpec((1, 1, VT_ROWS, META_PAD), lambda b, g, i: (g, 0, 0, 0)),
            pl.BlockSpec((tq, width), qmap),
        ],
        out_specs=pl.BlockSpec((tq, width), qmap),
        out_shape=jax.ShapeDtypeStruct((m, D_A), jnp.bfloat16),
        scratch_shapes=[pltpu.VMEM((GROUP_A * tq // GQA_SLAB_WIDTH, VT_ROWS, GQA_SLAB_WIDTH), jnp.float32),
                        pltpu.VMEM((2, chunk, GQA_SLAB_WIDTH), jnp.float32)],
        name="global_attention",
        compiler_params=pltpu.CompilerParams(
            dimension_semantics=("arbitrary", "arbitrary", "arbitrary"),
            vmem_limit_bytes=VMEM_LIMIT),
    )(safe, qat, ka, vat, km, vmt, gate)


def _nbr_kernel(safe_ref, qt_ref, k_ref, vt_ref, km_ref, vmt_ref, bias_ref, g_ref, o_ref, *, n_tiles):
    step = pl.program_id(2)
    bf = jnp.bfloat16
    km = km_ref[:N_META, :]
    vmt = vmt_ref[0, 0]
    meta_pad = jnp.zeros((META_PAD - N_META, NB_Q), bf)
    chunks_per_window = NB_K // NB_VCHUNK

    def dot(a, b):
        return jnp.dot(a, b, preferred_element_type=jnp.float32)

    def run(probs_fn):
        tiles = []
        for u in range(NB_TILES_PER_STEP):
            t = step * NB_TILES_PER_STEP + u
            cls = jnp.where(t == 0, 0, jnp.where(t == n_tiles - 1, 2, 1))
            first_chunk = jnp.clip(t - 1, 0, n_tiles - chunks_per_window)
            tiles.append((u, cls, first_chunk))
        scores = []
        for u, cls, first_chunk in tiles:
            qt = qt_ref[0, :, u * NB_Q:(u + 1) * NB_Q]
            kstart = pl.multiple_of(first_chunk * NB_VCHUNK, NB_VCHUNK)
            s_win = dot(k_ref[pl.ds(kstart, NB_K), :], qt) + bias_ref[0, cls]
            scores.append((s_win, dot(km, qt)))
        probs = [probs_fn(s_win, s_meta) for s_win, s_meta in scores]
        for (u, cls, first_chunk), (p_win, p_meta) in zip(tiles, probs):
            acc = dot(vmt, jnp.concatenate([p_meta, meta_pad], axis=0))
            for j in range(chunks_per_window):
                acc += dot(vt_ref[0, first_chunk + j], p_win[j * NB_VCHUNK:(j + 1) * NB_VCHUNK])
            o = (acc[:HEAD_DIM] * (1.0 / acc[HEAD_DIM:HEAD_DIM + 1])).T
            rows = slice(u * NB_Q, (u + 1) * NB_Q)
            o_ref[rows, :] = (o * g_ref[rows, :]).astype(o_ref.dtype)

    def probs_safe(s_win, s_meta):
        return jnp.exp2(s_win).astype(bf), jnp.exp2(s_meta).astype(bf)

    def probs_max(s_win, s_meta):
        m = jnp.maximum(jnp.max(s_win, axis=0, keepdims=True), jnp.max(s_meta, axis=0, keepdims=True))
        return jnp.exp2(s_win - m).astype(bf), jnp.exp2(s_meta - m).astype(bf)

    safe = safe_ref[0] != 0

    @pl.when(safe)
    def _():
        run(probs_safe)

    @pl.when(jnp.logical_not(safe))
    def _():
        run(probs_max)


def _neighbourhood_attention(safe, qbt, kb, vbt, gate, km, vmt, bias, batch, seq):
    m = gate.shape[0]
    tqb = NB_TILES_PER_STEP * NB_Q
    nq = seq // tqb
    n_vchunks = seq // NB_VCHUNK
    assert vbt.shape[3] == NB_VCHUNK
    qmap = lambda b, h, i: (b * nq + i, h)
    return pl.pallas_call(
        functools.partial(_nbr_kernel, n_tiles=seq // NB_Q),
        grid=(batch, N_HEADS_B, nq),
        in_specs=[
            pl.BlockSpec(memory_space=pltpu.SMEM),
            pl.BlockSpec((1, HEAD_DIM, tqb), lambda b, h, i: (h, 0, b * nq + i)),
            pl.BlockSpec((seq, HEAD_DIM), lambda b, h, i: (b, h)),
            pl.BlockSpec((1, n_vchunks, VT_ROWS, NB_VCHUNK), lambda b, h, i: (h, b, 0, 0)),
            pl.BlockSpec((META_PAD, HEAD_DIM), lambda b, h, i: (0, h)),
            pl.BlockSpec((1, 1, VT_ROWS, META_PAD), lambda b, h, i: (h, 0, 0, 0)),
            pl.BlockSpec((1, NB_CLASSES, NB_K, NB_Q), lambda b, h, i: (h, 0, 0, 0)),
            pl.BlockSpec((tqb, HEAD_DIM), qmap),
        ],
        out_specs=pl.BlockSpec((tqb, HEAD_DIM), qmap),
        out_shape=jax.ShapeDtypeStruct((m, D_B), jnp.bfloat16),
        name="neighbourhood_attention",
        compiler_params=pltpu.CompilerParams(
            dimension_semantics=("arbitrary", "arbitrary", "arbitrary"),
            vmem_limit_bytes=VMEM_LIMIT),
    )(safe, qbt, kb, vbt, km, vmt, bias, gate)


def _outproj_kernel(x_ref, ya_ref, yb_ref, w_ref, o_ref):
    y = jnp.dot(ya_ref[...], w_ref[:D_A, :], preferred_element_type=jnp.float32)
    y = y + jnp.dot(yb_ref[...], w_ref[D_A:, :], preferred_element_type=jnp.float32)
    o_ref[...] = x_ref[...] + y


def _out_projection(x2d, ya, yb, w_out_bf16, tm):
    m = x2d.shape[0]
    row = lambda i: (i, 0)
    return pl.pallas_call(
        _outproj_kernel,
        grid=(m // tm,),
        in_specs=[
            pl.BlockSpec((tm, D_MODEL), row),
            pl.BlockSpec((tm, D_A), row),
            pl.BlockSpec((tm, D_B), row),
            pl.BlockSpec((D_A + D_B, D_MODEL), lambda i: (0, 0)),
        ],
        out_specs=pl.BlockSpec((tm, D_MODEL), row),
        out_shape=jax.ShapeDtypeStruct((m, D_MODEL), jnp.float32),
        name="out_projection",
        compiler_params=pltpu.CompilerParams(
            dimension_semantics=("arbitrary",),
            vmem_limit_bytes=VMEM_LIMIT),
    )(x2d, ya, yb, w_out_bf16)


def _rope_tables(rows, cols):
    half = HEAD_DIM // 2
    inv_freq = ROPE_THETA ** (-jnp.arange(0, half, 2, dtype=jnp.float32) / half)
    ang_r = rows.astype(jnp.float32)[:, None] * inv_freq[None, :]
    ang_c = cols.astype(jnp.float32)[:, None] * inv_freq[None, :]
    n_r, n_c = rows.shape[0], cols.shape[0]

    def grid(r_part, c_part):
        r_part = jnp.broadcast_to(r_part[:, None, :], (n_r, n_c, r_part.shape[-1]))
        c_part = jnp.broadcast_to(c_part[None, :, :], (n_r, n_c, c_part.shape[-1]))
        return jnp.concatenate([r_part, c_part], axis=-1).reshape(n_r * n_c, HEAD_DIM)

    cr, sr, cc, sc = jnp.cos(ang_r), jnp.sin(ang_r), jnp.cos(ang_c), jnp.sin(ang_c)
    zr, zc = jnp.zeros_like(sr), jnp.zeros_like(sc)
    cos = grid(jnp.concatenate([cr, cr], -1), jnp.concatenate([cc, cc], -1))
    sin_a = grid(jnp.concatenate([-sr, zr], -1), jnp.concatenate([-sc, zc], -1))
    sin_b = grid(jnp.concatenate([zr, sr], -1), jnp.concatenate([zc, sc], -1))
    return cos, sin_a, sin_b


def _nbr_bias_tables(rpb):
    rows = 32
    tile_r = np.array([0, 2 * NB_QROWS, rows - NB_QROWS])[:, None, None]
    dr = np.arange(NB_QROWS)[None, :, None]
    jj = np.arange(NB_KROWS)[None, None, :]
    k0 = np.clip(tile_r - NB_QROWS, 0, rows - NB_KROWS)
    r = tile_r + dr
    kr = k0 + jj
    r0 = np.clip(r - WIN_R // 2, 0, rows - WIN_R)
    row_valid = (kr >= r0) & (kr < r0 + WIN_R)
    off_r = np.clip(kr - r + (WIN_R - 1), 0, 2 * WIN_R - 2)
    n_off_c = 2 * WIN_C - 1
    pos = np.arange(GRID_W)
    onehot = (pos[None, :, None] - pos[None, None, :] + (WIN_C - 1)
              == np.arange(n_off_c)[:, None, None]).astype(np.float32)
    toeplitz = jnp.einsum('hro,okc->hrkc', rpb.astype(jnp.float32) * LOG2E, onehot,
                          precision=lax.Precision.HIGHEST)

    def assemble(toep_ref, out_ref):
        kc = lax.broadcasted_iota(jnp.int32, (GRID_W, GRID_W), 0)
        c = lax.broadcasted_iota(jnp.int32, (GRID_W, GRID_W), 1)
        c0 = jnp.clip(c - WIN_C // 2, 0, GRID_W - WIN_C)
        col_valid = (kc >= c0) & (kc < c0 + WIN_C)
        masked = jnp.full((GRID_W, GRID_W), MASK_VALUE, jnp.float32)
        for cls in range(NB_CLASSES):
            for j in range(NB_KROWS):
                blocks = [jnp.where(col_valid, toep_ref[0, int(off_r[cls, d, j])], MASK_VALUE)
                          if row_valid[cls, d, j] else masked for d in range(NB_QROWS)]
                out_ref[0, cls, j * GRID_W:(j + 1) * GRID_W, :] = jnp.concatenate(blocks, axis=1)

    return pl.pallas_call(
        assemble,
        grid=(N_HEADS_B,),
        in_specs=[pl.BlockSpec((1, 2 * WIN_R - 1, GRID_W, GRID_W), lambda h: (h, 0, 0, 0))],
        out_specs=pl.BlockSpec((1, NB_CLASSES, NB_K, NB_Q), lambda h: (h, 0, 0, 0)),
        out_shape=jax.ShapeDtypeStruct((N_HEADS_B, NB_CLASSES, NB_K, NB_Q), jnp.float32),
        name="nbr_bias_table",
    )(toeplitz)


def kernel(x_prompt, x_sample, meta_tokens, norm_w, w_in, q_norm_a, k_norm_a, q_norm_b, k_norm_b, rpb, w_out):
    w_in_bf = w_in[0].astype(jnp.bfloat16)
    w_out_bf = w_out[0].astype(jnp.bfloat16)
    nw = norm_w[0].reshape(1, D_MODEL)
    gains = [g[0].reshape(1, HEAD_DIM) for g in (q_norm_a, k_norm_a, q_norm_b, k_norm_b)]
    bias = _nbr_bias_tables(rpb[0])

    meta_tabs = _rope_tables(jnp.full((1,), -1, jnp.int32), jnp.arange(META_PAD, dtype=jnp.int32))
    meta_x = jnp.pad(meta_tokens, ((0, META_PAD - N_META), (0, 0)))
    score_bound_a = (HEAD_DIM * Q_SCALE * jnp.max(jnp.abs(q_norm_a[0])) * jnp.max(jnp.abs(k_norm_a[0])))
    safe_a = (score_bound_a <= SAFE_SCORE_BOUND).astype(jnp.int32).reshape(1)
    score_bound_b = (HEAD_DIM * Q_SCALE * jnp.max(jnp.abs(q_norm_b[0])) * jnp.max(jnp.abs(k_norm_b[0]))
                     + LOG2E * jnp.max(jnp.abs(rpb[0])))
    safe_b = (score_bound_b <= SAFE_SCORE_BOUND).astype(jnp.int32).reshape(1)
    meta_out = _in_projection(meta_x, META_PAD, META_PAD, META_PAD, nw, w_in_bf, gains, meta_tabs)
    _, km_a, vmt_a, _, _, km_b, vmt_b, _ = meta_out

    def encode(x):
        batch, seq, _ = x.shape
        x2d = x.reshape(batch * seq, D_MODEL)
        tabs = _rope_tables(jnp.arange(seq // GRID_W, dtype=jnp.int32),
                            jnp.arange(GRID_W, dtype=jnp.int32))
        qat, ka, vat, ga, qbt, kb, vbt, gb = _in_projection(
            x2d, seq, IN_ROW_TILE, GQA_KEY_CHUNK, nw, w_in_bf, gains, tabs)
        ya = _global_attention(safe_a, qat, ka, vat, ga, km_a, vmt_a, batch, seq, GQA_Q_TILE)
        yb = _neighbourhood_attention(safe_b, qbt, kb, vbt, gb, km_b, vmt_b, bias, batch, seq)
        y = _out_projection(x2d, ya, yb, w_out_bf, OUT_ROW_TILE)
        return y.reshape(batch, seq, D_MODEL)

    return (encode(x_prompt), encode(x_sample))
```

```python
import functools
import math

import jax
import jax.numpy as jnp
import numpy as np
from jax import lax
from jax.experimental import pallas as pl
from jax.experimental.pallas import tpu as pltpu

D_MODEL = 2048
HEAD_DIM = 128
N_HEADS_A = 8
N_KV_A = 2
GROUP_A = N_HEADS_A // N_KV_A
N_HEADS_B = 8
D_A = N_HEADS_A * HEAD_DIM
D_KV_A = N_KV_A * HEAD_DIM
D_B = N_HEADS_B * HEAD_DIM
D_IN = D_A + 2 * D_KV_A + D_A + 4 * D_B
N_META = 16
GRID_W = 64
WIN_R = 8
WIN_C = 16
ROPE_THETA = 10000.0
EPS = 1e-6

LOG2E = math.log2(math.e)
Q_SCALE = HEAD_DIM ** -0.5 * LOG2E
MASK_VALUE = -1e30

LANES = 128
BF16_SUBLANE_TILE = 16
VT_EXTRA_ROWS = BF16_SUBLANE_TILE
VT_ROWS = HEAD_DIM + VT_EXTRA_ROWS
SAFE_SCORE_BOUND = 60.0
META_PAD = LANES
VMEM_LIMIT = 56 * 1024 * 1024

TN = 512
N_COL_TILES = D_IN // TN
NB_QROWS = 4
NB_Q = NB_QROWS * GRID_W
NB_KROWS = 3 * NB_QROWS
NB_K = NB_KROWS * GRID_W
NB_CLASSES = 3

IN_ROW_TILE = 256
OUT_ROW_TILE = 512
GQA_KEY_CHUNK = 512
GQA_Q_TILE = 512
NB_TILES_PER_STEP = 8
NB_VCHUNK = IN_ROW_TILE


def _silu(z):
    return z * (1.0 / (1.0 + jnp.exp(-z)))


def _head_norm(a, gain):
    ms = jnp.mean(a * a, axis=-1, keepdims=True)
    return a * lax.rsqrt(ms + EPS) * gain


def _rope(y, cos, sin_a, sin_b):
    return y * cos + pltpu.roll(y, 96, 1) * sin_a + pltpu.roll(y, 32, 1) * sin_b


def _inproj_kernel(x_ref, nw_ref, w_ref, gqa_ref, gka_ref, gqb_ref, gkb_ref,
                   cos_ref, sa_ref, sb_ref,
                   qat_ref, ka_ref, vat_ref, za_ref, qbt_ref, kb_ref, vbt_ref, zb_ref,
                   xn_ref):
    x = x_ref[...]
    ms = jnp.mean(x * x, axis=-1, keepdims=True)
    xn_ref[...] = (x * lax.rsqrt(ms + EPS) * nw_ref[...]).astype(jnp.bfloat16)

    def column_tile(j):
        return jnp.dot(xn_ref[...], w_ref[:, j * TN:(j + 1) * TN],
                       preferred_element_type=jnp.float32)

    def normed_roped(a, gain_ref):
        return _rope(_head_norm(a, gain_ref[...]), cos_ref[...], sa_ref[...], sb_ref[...])

    def store_heads(ref, first, vals):
        for h, v in enumerate(vals):
            ref[:, (first + h) * HEAD_DIM:(first + h + 1) * HEAD_DIM] = v.astype(ref.dtype)

    heads_per_tile = TN // HEAD_DIM

    def store_values_t(ref, first, vals):
        ones_row = (lax.broadcasted_iota(jnp.int32, (VT_EXTRA_ROWS, vals[0].shape[0]), 0) == 0)
        for h, a in enumerate(vals):
            ref[first + h, 0, :HEAD_DIM, :] = a.T.astype(ref.dtype)
            ref[first + h, 0, HEAD_DIM:, :] = ones_row.astype(ref.dtype)

    def epilogue(j, acc):
        heads = [acc[:, h * HEAD_DIM:(h + 1) * HEAD_DIM] for h in range(heads_per_tile)]
        if j < 2:
            for h, a in enumerate(heads):
                q = normed_roped(a, gqa_ref) * Q_SCALE
                qat_ref[j * heads_per_tile + h] = q.T.astype(qat_ref.dtype)
        elif j == 2:
            store_heads(ka_ref, 0, [normed_roped(a, gka_ref) for a in heads[:N_KV_A]])
            store_values_t(vat_ref, 0, heads[N_KV_A:])
        elif j < 5:
            za_ref[:, (j - 3) * TN:(j - 2) * TN] = _silu(acc)
        elif j < 7:
            for h, a in enumerate(heads):
                q = _head_norm(a, gqb_ref[...]) * Q_SCALE
                qbt_ref[(j - 5) * heads_per_tile + h] = q.T.astype(qbt_ref.dtype)
        elif j < 9:
            store_heads(kb_ref, (j - 7) * heads_per_tile,
                        [_head_norm(a, gkb_ref[...]) for a in heads])
        elif j < 11:
            store_values_t(vbt_ref, (j - 9) * heads_per_tile, heads)
        else:
            zb_ref[:, (j - 11) * TN:(j - 10) * TN] = _silu(acc)

    acc = column_tile(0)
    for j in range(N_COL_TILES):
        nxt = column_tile(j + 1) if j + 1 < N_COL_TILES else None
        epilogue(j, acc)
        acc = nxt


def _in_projection(x2d, seq, tm, chunk, norm_w, w_in_bf16, gains, tabs):
    m = x2d.shape[0]
    n_tab = seq // tm
    sub = chunk // tm
    cos, sin_a, sin_b = tabs
    row = lambda i: (i, 0)
    const = lambda i: (0, 0)
    bf, f32 = jnp.bfloat16, jnp.float32
    out_shape = [
        jax.ShapeDtypeStruct((N_HEADS_A, HEAD_DIM, m), bf), jax.ShapeDtypeStruct((m, D_KV_A), bf),
        jax.ShapeDtypeStruct((N_KV_A, m // chunk, VT_ROWS, chunk), bf),
        jax.ShapeDtypeStruct((m, D_A), f32),
        jax.ShapeDtypeStruct((N_HEADS_B, HEAD_DIM, m), bf), jax.ShapeDtypeStruct((m, D_B), bf),
        jax.ShapeDtypeStruct((N_HEADS_B, m // tm, VT_ROWS, tm), bf),
        jax.ShapeDtypeStruct((m, D_B), f32),
    ]
    out_specs = [
        pl.BlockSpec((N_HEADS_A, HEAD_DIM, tm), lambda i: (0, 0, i)),
        pl.BlockSpec((tm, D_KV_A), row),
        pl.BlockSpec((N_KV_A, 1, VT_ROWS, tm), lambda i: (0, i // sub, 0, i % sub)),
        pl.BlockSpec((tm, D_A), row),
        pl.BlockSpec((N_HEADS_B, HEAD_DIM, tm), lambda i: (0, 0, i)),
        pl.BlockSpec((tm, D_B), row),
        pl.BlockSpec((N_HEADS_B, 1, VT_ROWS, tm), lambda i: (0, i, 0, 0)),
        pl.BlockSpec((tm, D_B), row),
    ]
    gain_spec = pl.BlockSpec((1, HEAD_DIM), const)
    tab_spec = pl.BlockSpec((tm, HEAD_DIM), lambda i: (i % n_tab, 0))
    return pl.pallas_call(
        _inproj_kernel,
        grid=(m // tm,),
        in_specs=[
            pl.BlockSpec((tm, D_MODEL), row),
            pl.BlockSpec((1, D_MODEL), const),
            pl.BlockSpec((D_MODEL, D_IN), const, pipeline_mode=pl.Buffered(1)),
            gain_spec, gain_spec, gain_spec, gain_spec,
            tab_spec, tab_spec, tab_spec,
        ],
        out_specs=out_specs,
        out_shape=out_shape,
        scratch_shapes=[pltpu.VMEM((tm, D_MODEL), bf)],
        name="in_projection",
        compiler_params=pltpu.CompilerParams(
            dimension_semantics=("arbitrary",),
            vmem_limit_bytes=VMEM_LIMIT),
    )(x2d, norm_w, w_in_bf16, *gains, cos, sin_a, sin_b)


def _nt_dot(a, b):
    return lax.dot_general(a, b, (((1,), (1,)), ((), ())), preferred_element_type=jnp.float32)


GQA_SLAB_WIDTH = 512


def _gqa_kernel(safe_ref, qt_ref, k_ref, vt_ref, km_ref, vmt_ref, g_ref, o_ref, acc_ref, s_ref):
    tq = qt_ref.shape[2]
    n_chunks, chunk = vt_ref.shape[1], vt_ref.shape[3]
    width = acc_ref.shape[2]
    n_slabs = acc_ref.shape[0]
    heads_per_slab = width // tq
    km = km_ref[:N_META, :]
    vmt = vmt_ref[0, 0]
    meta_pad = jnp.zeros((META_PAD - N_META, width), jnp.bfloat16)

    def meta_values(p):
        return dot(vmt, jnp.concatenate([p.astype(jnp.bfloat16), meta_pad], axis=0))
    qts = [jnp.concatenate([qt_ref[i * heads_per_slab + j] for j in range(heads_per_slab)], axis=1)
           if heads_per_slab > 1 else qt_ref[i] for i in range(n_slabs)]

    def dot(a, b):
        return jnp.dot(a, b, preferred_element_type=jnp.float32)

    def keys(c):
        return k_ref[pl.ds(pl.multiple_of(c * chunk, chunk), chunk), :]

    def attend(init, consume):
        s_ref[0] = dot(keys(0), qts[0])
        s_meta = [dot(km, qt) for qt in qts]
        state = tuple(init(i, s) for i, s in enumerate(s_meta))

        def body(c, state):
            state = list(state)
            k = keys(c)
            k_next = keys(jnp.minimum(c + 1, n_chunks - 1))
            vt = vt_ref[0, c]
            for i in range(n_slabs):
                nxt = dot(k, qts[i + 1]) if i + 1 < n_slabs else dot(k_next, qts[0])
                s_ref[(i + 1) % 2] = nxt
                state[i] = consume(i, s_ref[i % 2], vt, state[i])
            return tuple(state)

        lax.fori_loop(0, n_chunks, body, state)
        for i in range(n_slabs):
            acc = acc_ref[i]
            o_slab = acc[:HEAD_DIM] * (1.0 / acc[HEAD_DIM:HEAD_DIM + 1])
            for j in range(heads_per_slab):
                h = i * heads_per_slab + j
                sl = slice(h * HEAD_DIM, (h + 1) * HEAD_DIM)
                o = o_slab[:, j * tq:(j + 1) * tq].T
                o_ref[:, sl] = (o * g_ref[:, sl]).astype(o_ref.dtype)

    safe = safe_ref[0] != 0

    @pl.when(safe)
    def _():
        def init(i, s):
            acc_ref[i] = meta_values(jnp.exp2(s))
            return 0

        def consume(i, s, vt, state):
            acc_ref[i] += dot(vt, jnp.exp2(s).astype(jnp.bfloat16))
            return state

        attend(init, consume)

    @pl.when(jnp.logical_not(safe))
    def _():
        def init(i, s):
            m = jnp.max(s, axis=0, keepdims=True)
            acc_ref[i] = meta_values(jnp.exp2(s - m))
            return m

        def consume(i, s, vt, m):
            m_new = jnp.maximum(m, jnp.max(s, axis=0, keepdims=True))
            p = jnp.exp2(s - m_new)
            acc_ref[i] = jnp.exp2(m - m_new) * acc_ref[i] + dot(vt, p.astype(jnp.bfloat16))
            return m_new

        attend(init, consume)


def _global_attention(safe, qat, ka, vat, gate, km, vmt, batch, seq, tq):
    m = gate.shape[0]
    nq = seq // tq
    chunk = vat.shape[3]
    n_chunks = seq // chunk
    qmap = lambda b, g, i: (b * nq + i, g)
    width = GROUP_A * HEAD_DIM
    return pl.pallas_call(
        _gqa_kernel,
        grid=(batch, N_KV_A, nq),
        in_specs=[
            pl.BlockSpec(memory_space=pltpu.SMEM),
            pl.BlockSpec((GROUP_A, HEAD_DIM, tq), lambda b, g, i: (g, 0, b * nq + i)),
            pl.BlockSpec((seq, HEAD_DIM), lambda b, g, i: (b, g)),
            pl.BlockSpec((1, n_chunks, VT_ROWS, chunk), lambda b, g, i: (g, b, 0, 0)),
            pl.BlockSpec((META_PAD, HEAD_DIM), lambda b, g, i: (0, g)),
            pl.BlockSpec((1, 1, VT_ROWS, META_PAD), lambda b, g, i: (g, 0, 0, 0)),
            pl.BlockSpec((tq, width), qmap),
        ],
        out_specs=pl.BlockSpec((tq, width), qmap),
        out_shape=jax.ShapeDtypeStruct((m, D_A), jnp.bfloat16),
        scratch_shapes=[pltpu.VMEM((GROUP_A * tq // GQA_SLAB_WIDTH, VT_ROWS, GQA_SLAB_WIDTH), jnp.float32),
                        pltpu.VMEM((2, chunk, GQA_SLAB_WIDTH), jnp.float32)],
        name="global_attention",
        compiler_params=pltpu.CompilerParams(
            dimension_semantics=("arbitrary", "arbitrary", "arbitrary"),
            vmem_limit_bytes=VMEM_LIMIT),
    )(safe, qat, ka, vat, km, vmt, gate)


def _nbr_kernel(safe_ref, qt_ref, k_ref, vt_ref, km_ref, vmt_ref, bias_ref, g_ref, o_ref, *, n_tiles):
    step = pl.program_id(2)
    bf = jnp.bfloat16
    km = km_ref[:N_META, :]
    vmt = vmt_ref[0, 0]
    meta_pad = jnp.zeros((META_PAD - N_META, NB_Q), bf)
    chunks_per_window = NB_K // NB_VCHUNK

    def dot(a, b):
        return jnp.dot(a, b, preferred_element_type=jnp.float32)

    def run(probs_fn):
        tiles = []
        for u in range(NB_TILES_PER_STEP):
            t = step * NB_TILES_PER_STEP + u
            cls = jnp.where(t == 0, 0, jnp.where(t == n_tiles - 1, 2, 1))
            first_chunk = jnp.clip(t - 1, 0, n_tiles - chunks_per_window)
            tiles.append((u, cls, first_chunk))
        scores = []
        for u, cls, first_chunk in tiles:
            qt = qt_ref[0, :, u * NB_Q:(u + 1) * NB_Q]
            kstart = pl.multiple_of(first_chunk * NB_VCHUNK, NB_VCHUNK)
            s_win = dot(k_ref[pl.ds(kstart, NB_K), :], qt) + bias_ref[0, cls]
            scores.append((s_win, dot(km, qt)))
        probs = [probs_fn(s_win, s_meta) for s_win, s_meta in scores]
        for (u, cls, first_chunk), (p_win, p_meta) in zip(tiles, probs):
            acc = dot(vmt, jnp.concatenate([p_meta, meta_pad], axis=0))
            for j in range(chunks_per_window):
                acc += dot(vt_ref[0, first_chunk + j], p_win[j * NB_VCHUNK:(j + 1) * NB_VCHUNK])
            o = (acc[:HEAD_DIM] * (1.0 / acc[HEAD_DIM:HEAD_DIM + 1])).T
            rows = slice(u * NB_Q, (u + 1) * NB_Q)
            o_ref[rows, :] = (o * g_ref[rows, :]).astype(o_ref.dtype)

    def probs_safe(s_win, s_meta):
        return jnp.exp2(s_win).astype(bf), jnp.exp2(s_meta).astype(bf)

    def probs_max(s_win, s_meta):
        m = jnp.maximum(jnp.max(s_win, axis=0, keepdims=True), jnp.max(s_meta, axis=0, keepdims=True))
        return jnp.exp2(s_win - m).astype(bf), jnp.exp2(s_meta - m).astype(bf)

    safe = safe_ref[0] != 0

    @pl.when(safe)
    def _():
        run(probs_safe)

    @pl.when(jnp.logical_not(safe))
    def _():
        run(probs_max)


def _neighbourhood_attention(safe, qbt, kb, vbt, gate, km, vmt, bias, batch, seq):
    m = gate.shape[0]
    tqb = NB_TILES_PER_STEP * NB_Q
    nq = seq // tqb
    n_vchunks = seq // NB_VCHUNK
    assert vbt.shape[3] == NB_VCHUNK
    qmap = lambda h, b, i: (b * nq + i, h)
    return pl.pallas_call(
        functools.partial(_nbr_kernel, n_tiles=seq // NB_Q),
        grid=(N_HEADS_B, batch, nq),
        in_specs=[
            pl.BlockSpec(memory_space=pltpu.SMEM),
            pl.BlockSpec((1, HEAD_DIM, tqb), lambda h, b, i: (h, 0, b * nq + i)),
            pl.BlockSpec((seq, HEAD_DIM), lambda h, b, i: (b, h)),
            pl.BlockSpec((1, n_vchunks, VT_ROWS, NB_VCHUNK), lambda h, b, i: (h, b, 0, 0)),
            pl.BlockSpec((META_PAD, HEAD_DIM), lambda h, b, i: (0, h)),
            pl.BlockSpec((1, 1, VT_ROWS, META_PAD), lambda h, b, i: (h, 0, 0, 0)),
            pl.BlockSpec((1, NB_CLASSES, NB_K, NB_Q), lambda h, b, i: (h, 0, 0, 0)),
            pl.BlockSpec((tqb, HEAD_DIM), qmap),
        ],
        out_specs=pl.BlockSpec((tqb, HEAD_DIM), qmap),
        out_shape=jax.ShapeDtypeStruct((m, D_B), jnp.bfloat16),
        name="neighbourhood_attention",
        compiler_params=pltpu.CompilerParams(
            dimension_semantics=("arbitrary", "arbitrary", "arbitrary"),
            vmem_limit_bytes=VMEM_LIMIT),
    )(safe, qbt, kb, vbt, km, vmt, bias, gate)


def _outproj_kernel(x_ref, ya_ref, yb_ref, w_ref, o_ref):
    y = jnp.dot(ya_ref[...], w_ref[:D_A, :], preferred_element_type=jnp.float32)
    y = y + jnp.dot(yb_ref[...], w_ref[D_A:, :], preferred_element_type=jnp.float32)
    o_ref[...] = x_ref[...] + y


def _out_projection(x2d, ya, yb, w_out_bf16, tm):
    m = x2d.shape[0]
    row = lambda i: (i, 0)
    return pl.pallas_call(
        _outproj_kernel,
        grid=(m // tm,),
        in_specs=[
            pl.BlockSpec((tm, D_MODEL), row),
            pl.BlockSpec((tm, D_A), row),
            pl.BlockSpec((tm, D_B), row),
            pl.BlockSpec((D_A + D_B, D_MODEL), lambda i: (0, 0)),
        ],
        out_specs=pl.BlockSpec((tm, D_MODEL), row),
        out_shape=jax.ShapeDtypeStruct((m, D_MODEL), jnp.float32),
        name="out_projection",
        compiler_params=pltpu.CompilerParams(
            dimension_semantics=("arbitrary",),
            vmem_limit_bytes=VMEM_LIMIT),
    )(x2d, ya, yb, w_out_bf16)


def _rope_tables(rows, cols):
    half = HEAD_DIM // 2
    inv_freq = ROPE_THETA ** (-jnp.arange(0, half, 2, dtype=jnp.float32) / half)
    ang_r = rows.astype(jnp.float32)[:, None] * inv_freq[None, :]
    ang_c = cols.astype(jnp.float32)[:, None] * inv_freq[None, :]
    n_r, n_c = rows.shape[0], cols.shape[0]

    def grid(r_part, c_part):
        r_part = jnp.broadcast_to(r_part[:, None, :], (n_r, n_c, r_part.shape[-1]))
        c_part = jnp.broadcast_to(c_part[None, :, :], (n_r, n_c, c_part.shape[-1]))
        return jnp.concatenate([r_part, c_part], axis=-1).reshape(n_r * n_c, HEAD_DIM)

    cr, sr, cc, sc = jnp.cos(ang_r), jnp.sin(ang_r), jnp.cos(ang_c), jnp.sin(ang_c)
    zr, zc = jnp.zeros_like(sr), jnp.zeros_like(sc)
    cos = grid(jnp.concatenate([cr, cr], -1), jnp.concatenate([cc, cc], -1))
    sin_a = grid(jnp.concatenate([-sr, zr], -1), jnp.concatenate([-sc, zc], -1))
    sin_b = grid(jnp.concatenate([zr, sr], -1), jnp.concatenate([zc, sc], -1))
    return cos, sin_a, sin_b


def _nbr_bias_tables(rpb):
    rows = 32
    tile_r = np.array([0, 2 * NB_QROWS, rows - NB_QROWS])[:, None, None]
    dr = np.arange(NB_QROWS)[None, :, None]
    jj = np.arange(NB_KROWS)[None, None, :]
    k0 = np.clip(tile_r - NB_QROWS, 0, rows - NB_KROWS)
    r = tile_r + dr
    kr = k0 + jj
    r0 = np.clip(r - WIN_R // 2, 0, rows - WIN_R)
    row_valid = (kr >= r0) & (kr < r0 + WIN_R)
    off_r = np.clip(kr - r + (WIN_R - 1), 0, 2 * WIN_R - 2)
    n_off_c = 2 * WIN_C - 1
    pos = np.arange(GRID_W)
    onehot = (pos[None, :, None] - pos[None, None, :] + (WIN_C - 1)
              == np.arange(n_off_c)[:, None, None]).astype(np.float32)
    toeplitz = jnp.einsum('hro,okc->hrkc', rpb.astype(jnp.float32) * LOG2E, onehot,
                          precision=lax.Precision.HIGHEST)

    def assemble(toep_ref, out_ref):
        kc = lax.broadcasted_iota(jnp.int32, (GRID_W, GRID_W), 0)
        c = lax.broadcasted_iota(jnp.int32, (GRID_W, GRID_W), 1)
        c0 = jnp.clip(c - WIN_C // 2, 0, GRID_W - WIN_C)
        col_valid = (kc >= c0) & (kc < c0 + WIN_C)
        masked = jnp.full((GRID_W, GRID_W), MASK_VALUE, jnp.float32)
        for cls in range(NB_CLASSES):
            for j in range(NB_KROWS):
                blocks = [jnp.where(col_valid, toep_ref[0, int(off_r[cls, d, j])], MASK_VALUE)
                          if row_valid[cls, d, j] else masked for d in range(NB_QROWS)]
                out_ref[0, cls, j * GRID_W:(j + 1) * GRID_W, :] = jnp.concatenate(blocks, axis=1)

    return pl.pallas_call(
        assemble,
        grid=(N_HEADS_B,),
        in_specs=[pl.BlockSpec((1, 2 * WIN_R - 1, GRID_W, GRID_W), lambda h: (h, 0, 0, 0))],
        out_specs=pl.BlockSpec((1, NB_CLASSES, NB_K, NB_Q), lambda h: (h, 0, 0, 0)),
        out_shape=jax.ShapeDtypeStruct((N_HEADS_B, NB_CLASSES, NB_K, NB_Q), jnp.float32),
        name="nbr_bias_table",
    )(toeplitz)


def kernel(x_prompt, x_sample, meta_tokens, norm_w, w_in, q_norm_a, k_norm_a, q_norm_b, k_norm_b, rpb, w_out):
    w_in_bf = w_in[0].astype(jnp.bfloat16)
    w_out_bf = w_out[0].astype(jnp.bfloat16)
    nw = norm_w[0].reshape(1, D_MODEL)
    gains = [g[0].reshape(1, HEAD_DIM) for g in (q_norm_a, k_norm_a, q_norm_b, k_norm_b)]
    bias = _nbr_bias_tables(rpb[0])

    meta_tabs = _rope_tables(jnp.full((1,), -1, jnp.int32), jnp.arange(META_PAD, dtype=jnp.int32))
    meta_x = jnp.pad(meta_tokens, ((0, META_PAD - N_META), (0, 0)))
    score_bound_a = (HEAD_DIM * Q_SCALE * jnp.max(jnp.abs(q_norm_a[0])) * jnp.max(jnp.abs(k_norm_a[0])))
    safe_a = (score_bound_a <= SAFE_SCORE_BOUND).astype(jnp.int32).reshape(1)
    score_bound_b = (HEAD_DIM * Q_SCALE * jnp.max(jnp.abs(q_norm_b[0])) * jnp.max(jnp.abs(k_norm_b[0]))
                     + LOG2E * jnp.max(jnp.abs(rpb[0])))
    safe_b = (score_bound_b <= SAFE_SCORE_BOUND).astype(jnp.int32).reshape(1)
    meta_out = _in_projection(meta_x, META_PAD, META_PAD, META_PAD, nw, w_in_bf, gains, meta_tabs)
    _, km_a, vmt_a, _, _, km_b, vmt_b, _ = meta_out

    def encode(x):
        batch, seq, _ = x.shape
        x2d = x.reshape(batch * seq, D_MODEL)
        tabs = _rope_tables(jnp.arange(seq // GRID_W, dtype=jnp.int32),
                            jnp.arange(GRID_W, dtype=jnp.int32))
        qat, ka, vat, ga, qbt, kb, vbt, gb = _in_projection(
            x2d, seq, IN_ROW_TILE, GQA_KEY_CHUNK, nw, w_in_bf, gains, tabs)
        ya = _global_attention(safe_a, qat, ka, vat, ga, km_a, vmt_a, batch, seq, GQA_Q_TILE)
        yb = _neighbourhood_attention(safe_b, qbt, kb, vbt, gb, km_b, vmt_b, bias, batch, seq)
        y = _out_projection(x2d, ya, yb, w_out_bf, OUT_ROW_TILE)
        return y.reshape(batch, seq, D_MODEL)

    return (encode(x_prompt), encode(x_sample))
```

```python
import functools
import math

import jax
import jax.numpy as jnp
import numpy as np
from jax import lax
from jax.experimental import pallas as pl
from jax.experimental.pallas import tpu as pltpu

D_MODEL = 2048
HEAD_DIM = 128
N_HEADS_A = 8
N_KV_A = 2
GROUP_A = N_HEADS_A // N_KV_A
N_HEADS_B = 8
D_A = N_HEADS_A * HEAD_DIM
D_KV_A = N_KV_A * HEAD_DIM
D_B = N_HEADS_B * HEAD_DIM
D_IN = D_A + 2 * D_KV_A + D_A + 4 * D_B
N_META = 16
GRID_W = 64
WIN_R = 8
WIN_C = 16
ROPE_THETA = 10000.0
EPS = 1e-6

LOG2E = math.log2(math.e)
Q_SCALE = HEAD_DIM ** -0.5 * LOG2E
MASK_VALUE = -1e30

LANES = 128
BF16_SUBLANE_TILE = 16
VT_EXTRA_ROWS = BF16_SUBLANE_TILE
VT_ROWS = HEAD_DIM + VT_EXTRA_ROWS
SAFE_SCORE_BOUND = 60.0
META_PAD = LANES
VMEM_LIMIT = 56 * 1024 * 1024

TN = 512
N_COL_TILES = D_IN // TN
NB_QROWS = 4
NB_Q = NB_QROWS * GRID_W
NB_KROWS = 3 * NB_QROWS
NB_K = NB_KROWS * GRID_W
NB_CLASSES = 3

IN_ROW_TILE = 256
OUT_ROW_TILE = 512
GQA_KEY_CHUNK = 1024
GQA_Q_TILE = 512
NB_TILES_PER_STEP = 8
NB_VCHUNK = IN_ROW_TILE


def _silu(z):
    return z * (1.0 / (1.0 + jnp.exp(-z)))


def _head_norm(a, gain):
    ms = jnp.mean(a * a, axis=-1, keepdims=True)
    return a * lax.rsqrt(ms + EPS) * gain


def _rope(y, cos, sin_a, sin_b):
    return y * cos + pltpu.roll(y, 96, 1) * sin_a + pltpu.roll(y, 32, 1) * sin_b


def _inproj_kernel(x_ref, nw_ref, w_ref, gqa_ref, gka_ref, gqb_ref, gkb_ref,
                   cos_ref, sa_ref, sb_ref,
                   qat_ref, ka_ref, vat_ref, za_ref, qbt_ref, kb_ref, vbt_ref, zb_ref,
                   xn_ref):
    x = x_ref[...]
    ms = jnp.mean(x * x, axis=-1, keepdims=True)
    xn_ref[...] = (x * lax.rsqrt(ms + EPS) * nw_ref[...]).astype(jnp.bfloat16)

    def column_tile(j):
        return jnp.dot(xn_ref[...], w_ref[:, j * TN:(j + 1) * TN],
                       preferred_element_type=jnp.float32)

    def normed_roped(a, gain_ref):
        return _rope(_head_norm(a, gain_ref[...]), cos_ref[...], sa_ref[...], sb_ref[...])

    def store_heads(ref, first, vals):
        for h, v in enumerate(vals):
            ref[:, (first + h) * HEAD_DIM:(first + h + 1) * HEAD_DIM] = v.astype(ref.dtype)

    heads_per_tile = TN // HEAD_DIM

    def store_values_t(ref, first, vals):
        ones_row = (lax.broadcasted_iota(jnp.int32, (VT_EXTRA_ROWS, vals[0].shape[0]), 0) == 0)
        for h, a in enumerate(vals):
            ref[first + h, 0, :HEAD_DIM, :] = a.T.astype(ref.dtype)
            ref[first + h, 0, HEAD_DIM:, :] = ones_row.astype(ref.dtype)

    def epilogue(j, acc):
        heads = [acc[:, h * HEAD_DIM:(h + 1) * HEAD_DIM] for h in range(heads_per_tile)]
        if j < 2:
            for h, a in enumerate(heads):
                q = normed_roped(a, gqa_ref) * Q_SCALE
                qat_ref[j * heads_per_tile + h] = q.T.astype(qat_ref.dtype)
        elif j == 2:
            store_heads(ka_ref, 0, [normed_roped(a, gka_ref) for a in heads[:N_KV_A]])
            store_values_t(vat_ref, 0, heads[N_KV_A:])
        elif j < 5:
            za_ref[:, (j - 3) * TN:(j - 2) * TN] = _silu(acc)
        elif j < 7:
            for h, a in enumerate(heads):
                q = _head_norm(a, gqb_ref[...]) * Q_SCALE
                qbt_ref[(j - 5) * heads_per_tile + h] = q.T.astype(qbt_ref.dtype)
        elif j < 9:
            store_heads(kb_ref, (j - 7) * heads_per_tile,
                        [_head_norm(a, gkb_ref[...]) for a in heads])
        elif j < 11:
            store_values_t(vbt_ref, (j - 9) * heads_per_tile, heads)
        else:
            zb_ref[:, (j - 11) * TN:(j - 10) * TN] = _silu(acc)

    acc = column_tile(0)
    for j in range(N_COL_TILES):
        nxt = column_tile(j + 1) if j + 1 < N_COL_TILES else None
        epilogue(j, acc)
        acc = nxt


def _in_projection(x2d, seq, tm, chunk, norm_w, w_in_bf16, gains, tabs):
    m = x2d.shape[0]
    n_tab = seq // tm
    sub = chunk // tm
    cos, sin_a, sin_b = tabs
    row = lambda i: (i, 0)
    const = lambda i: (0, 0)
    bf, f32 = jnp.bfloat16, jnp.float32
    out_shape = [
        jax.ShapeDtypeStruct((N_HEADS_A, HEAD_DIM, m), bf), jax.ShapeDtypeStruct((m, D_KV_A), bf),
        jax.ShapeDtypeStruct((N_KV_A, m // chunk, VT_ROWS, chunk), bf),
        jax.ShapeDtypeStruct((m, D_A), f32),
        jax.ShapeDtypeStruct((N_HEADS_B, HEAD_DIM, m), bf), jax.ShapeDtypeStruct((m, D_B), bf),
        jax.ShapeDtypeStruct((N_HEADS_B, m // tm, VT_ROWS, tm), bf),
        jax.ShapeDtypeStruct((m, D_B), f32),
    ]
    out_specs = [
        pl.BlockSpec((N_HEADS_A, HEAD_DIM, tm), lambda i: (0, 0, i)),
        pl.BlockSpec((tm, D_KV_A), row),
        pl.BlockSpec((N_KV_A, 1, VT_ROWS, tm), lambda i: (0, i // sub, 0, i % sub)),
        pl.BlockSpec((tm, D_A), row),
        pl.BlockSpec((N_HEADS_B, HEAD_DIM, tm), lambda i: (0, 0, i)),
        pl.BlockSpec((tm, D_B), row),
        pl.BlockSpec((N_HEADS_B, 1, VT_ROWS, tm), lambda i: (0, i, 0, 0)),
        pl.BlockSpec((tm, D_B), row),
    ]
    gain_spec = pl.BlockSpec((1, HEAD_DIM), const)
    tab_spec = pl.BlockSpec((tm, HEAD_DIM), lambda i: (i % n_tab, 0))
    return pl.pallas_call(
        _inproj_kernel,
        grid=(m // tm,),
        in_specs=[
            pl.BlockSpec((tm, D_MODEL), row),
            pl.BlockSpec((1, D_MODEL), const),
            pl.BlockSpec((D_MODEL, D_IN), const, pipeline_mode=pl.Buffered(1)),
            gain_spec, gain_spec, gain_spec, gain_spec,
            tab_spec, tab_spec, tab_spec,
        ],
        out_specs=out_specs,
        out_shape=out_shape,
        scratch_shapes=[pltpu.VMEM((tm, D_MODEL), bf)],
        name="in_projection",
        compiler_params=pltpu.CompilerParams(
            dimension_semantics=("arbitrary",),
            vmem_limit_bytes=VMEM_LIMIT),
    )(x2d, norm_w, w_in_bf16, *gains, cos, sin_a, sin_b)


def _nt_dot(a, b):
    return lax.dot_general(a, b, (((1,), (1,)), ((), ())), preferred_element_type=jnp.float32)


GQA_SLAB_WIDTH = 512


def _gqa_kernel(safe_ref, qt_ref, k_ref, vt_ref, km_ref, vmt_ref, g_ref, o_ref, acc_ref, s_ref):
    tq = qt_ref.shape[2]
    n_chunks, chunk = vt_ref.shape[1], vt_ref.shape[3]
    width = acc_ref.shape[2]
    n_slabs = acc_ref.shape[0]
    heads_per_slab = width // tq
    km = km_ref[:N_META, :]
    vmt = vmt_ref[0, 0]
    meta_pad = jnp.zeros((META_PAD - N_META, width), jnp.bfloat16)

    def meta_values(p):
        return dot(vmt, jnp.concatenate([p.astype(jnp.bfloat16), meta_pad], axis=0))
    qts = [jnp.concatenate([qt_ref[i * heads_per_slab + j] for j in range(heads_per_slab)], axis=1)
           if heads_per_slab > 1 else qt_ref[i] for i in range(n_slabs)]

    def dot(a, b):
        return jnp.dot(a, b, preferred_element_type=jnp.float32)

    def keys(c):
        return k_ref[pl.ds(pl.multiple_of(c * chunk, chunk), chunk), :]

    def attend(init, consume):
        s_ref[0] = dot(keys(0), qts[0])
        s_meta = [dot(km, qt) for qt in qts]
        state = tuple(init(i, s) for i, s in enumerate(s_meta))

        def body(c, state):
            state = list(state)
            k = keys(c)
            k_next = keys(jnp.minimum(c + 1, n_chunks - 1))
            vt = vt_ref[0, c]
            for i in range(n_slabs):
                nxt = dot(k, qts[i + 1]) if i + 1 < n_slabs else dot(k_next, qts[0])
                s_ref[(i + 1) % 2] = nxt
                state[i] = consume(i, s_ref[i % 2], vt, state[i])
            return tuple(state)

        lax.fori_loop(0, n_chunks, body, state)
        for i in range(n_slabs):
            acc = acc_ref[i]
            o_slab = acc[:HEAD_DIM] * (1.0 / acc[HEAD_DIM:HEAD_DIM + 1])
            for j in range(heads_per_slab):
                h = i * heads_per_slab + j
                sl = slice(h * HEAD_DIM, (h + 1) * HEAD_DIM)
                o = o_slab[:, j * tq:(j + 1) * tq].T
                o_ref[:, sl] = (o * g_ref[:, sl]).astype(o_ref.dtype)

    safe = safe_ref[0] != 0

    @pl.when(safe)
    def _():
        def init(i, s):
            acc_ref[i] = meta_values(jnp.exp2(s))
            return 0

        def consume(i, s, vt, state):
            acc_ref[i] += dot(vt, jnp.exp2(s).astype(jnp.bfloat16))
            return state

        attend(init, consume)

    @pl.when(jnp.logical_not(safe))
    def _():
        def init(i, s):
            m = jnp.max(s, axis=0, keepdims=True)
            acc_ref[i] = meta_values(jnp.exp2(s - m))
            return m

        def consume(i, s, vt, m):
            m_new = jnp.maximum(m, jnp.max(s, axis=0, keepdims=True))
            p = jnp.exp2(s - m_new)
            acc_ref[i] = jnp.exp2(m - m_new) * acc_ref[i] + dot(vt, p.astype(jnp.bfloat16))
            return m_new

        attend(init, consume)


def _global_attention(safe, qat, ka, vat, gate, km, vmt, batch, seq, tq):
    m = gate.shape[0]
    nq = seq // tq
    chunk = vat.shape[3]
    n_chunks = seq // chunk
    qmap = lambda b, g, i: (b * nq + i, g)
    width = GROUP_A * HEAD_DIM
    return pl.pallas_call(
        _gqa_kernel,
        grid=(batch, N_KV_A, nq),
        in_specs=[
            pl.BlockSpec(memory_space=pltpu.SMEM),
            pl.BlockSpec((GROUP_A, HEAD_DIM, tq), lambda b, g, i: (g, 0, b * nq + i)),
            pl.BlockSpec((seq, HEAD_DIM), lambda b, g, i: (b, g)),
            pl.BlockSpec((1, n_chunks, VT_ROWS, chunk), lambda b, g, i: (g, b, 0, 0)),
            pl.BlockSpec((META_PAD, HEAD_DIM), lambda b, g, i: (0, g)),
            pl.BlockSpec((1, 1, VT_ROWS, META_PAD), lambda b, g, i: (g, 0, 0, 0)),
            pl.BlockSpec((tq, width), qmap),
        ],
        out_specs=pl.BlockSpec((tq, width), qmap),
        out_shape=jax.ShapeDtypeStruct((m, D_A), jnp.bfloat16),
        scratch_shapes=[pltpu.VMEM((GROUP_A * tq // GQA_SLAB_WIDTH, VT_ROWS, GQA_SLAB_WIDTH), jnp.float32),
                        pltpu.VMEM((2, chunk, GQA_SLAB_WIDTH), jnp.float32)],
        name="global_attention",
        compiler_params=pltpu.CompilerParams(
            dimension_semantics=("arbitrary", "arbitrary", "arbitrary"),
            vmem_limit_bytes=VMEM_LIMIT),
    )(safe, qat, ka, vat, km, vmt, gate)


def _nbr_kernel(safe_ref, qt_ref, k_ref, vt_ref, km_ref, vmt_ref, bias_ref, g_ref, o_ref, *, n_tiles):
    step = pl.program_id(1)
    bf = jnp.bfloat16
    km = km_ref[:N_META, :]
    vmt = vmt_ref[0, 0]
    meta_pad = jnp.zeros((META_PAD - N_META, NB_Q), bf)
    chunks_per_window = NB_K // NB_VCHUNK

    def dot(a, b):
        return jnp.dot(a, b, preferred_element_type=jnp.float32)

    def run(probs_fn):
        tiles = []
        for u in range(NB_TILES_PER_STEP):
            tile = step * NB_TILES_PER_STEP + u
            seq_first = (tile // n_tiles) * n_tiles
            t = tile - seq_first
            cls = jnp.where(t == 0, 0, jnp.where(t == n_tiles - 1, 2, 1))
            first_chunk = seq_first + jnp.clip(t - 1, 0, n_tiles - chunks_per_window)
            tiles.append((u, cls, first_chunk))
        scores = []
        for u, cls, first_chunk in tiles:
            qt = qt_ref[0, :, u * NB_Q:(u + 1) * NB_Q]
            kstart = pl.multiple_of(first_chunk * NB_VCHUNK, NB_VCHUNK)
            s_win = dot(k_ref[pl.ds(kstart, NB_K), :], qt) + bias_ref[0, cls]
            scores.append((s_win, dot(km, qt)))
        probs = [probs_fn(s_win, s_meta) for s_win, s_meta in scores]
        for (u, cls, first_chunk), (p_win, p_meta) in zip(tiles, probs):
            acc = dot(vmt, jnp.concatenate([p_meta, meta_pad], axis=0))
            for j in range(chunks_per_window):
                acc += dot(vt_ref[0, first_chunk + j], p_win[j * NB_VCHUNK:(j + 1) * NB_VCHUNK])
            o = (acc[:HEAD_DIM] * (1.0 / acc[HEAD_DIM:HEAD_DIM + 1])).T
            rows = slice(u * NB_Q, (u + 1) * NB_Q)
            o_ref[rows, :] = (o * g_ref[rows, :]).astype(o_ref.dtype)

    def probs_safe(s_win, s_meta):
        return jnp.exp2(s_win).astype(bf), jnp.exp2(s_meta).astype(bf)

    def probs_max(s_win, s_meta):
        m = jnp.maximum(jnp.max(s_win, axis=0, keepdims=True), jnp.max(s_meta, axis=0, keepdims=True))
        return jnp.exp2(s_win - m).astype(bf), jnp.exp2(s_meta - m).astype(bf)

    safe = safe_ref[0] != 0

    @pl.when(safe)
    def _():
        run(probs_safe)

    @pl.when(jnp.logical_not(safe))
    def _():
        run(probs_max)


def _neighbourhood_attention(safe, qbt, kb, vbt, gate, km, vmt, bias, seq):
    m = gate.shape[0]
    tqb = NB_TILES_PER_STEP * NB_Q
    assert vbt.shape[3] == NB_VCHUNK and seq % tqb == 0
    qmap = lambda h, i: (i, h)
    per_head = lambda h, i: (h, 0, 0, 0)
    return pl.pallas_call(
        functools.partial(_nbr_kernel, n_tiles=seq // NB_Q),
        grid=(N_HEADS_B, m // tqb),
        in_specs=[
            pl.BlockSpec(memory_space=pltpu.SMEM),
            pl.BlockSpec((1, HEAD_DIM, tqb), lambda h, i: (h, 0, i)),
            pl.BlockSpec((m, HEAD_DIM), lambda h, i: (0, h)),
            pl.BlockSpec((1, m // NB_VCHUNK, VT_ROWS, NB_VCHUNK), per_head),
            pl.BlockSpec((META_PAD, HEAD_DIM), lambda h, i: (0, h)),
            pl.BlockSpec((1, 1, VT_ROWS, META_PAD), per_head),
            pl.BlockSpec((1, NB_CLASSES, NB_K, NB_Q), per_head),
            pl.BlockSpec((tqb, HEAD_DIM), qmap),
        ],
        out_specs=pl.BlockSpec((tqb, HEAD_DIM), qmap),
        out_shape=jax.ShapeDtypeStruct((m, D_B), jnp.bfloat16),
        name="neighbourhood_attention",
        compiler_params=pltpu.CompilerParams(
            dimension_semantics=("arbitrary", "arbitrary"),
            vmem_limit_bytes=VMEM_LIMIT),
    )(safe, qbt, kb, vbt, km, vmt, bias, gate)


def _outproj_kernel(x_ref, ya_ref, yb_ref, w_ref, o_ref):
    y = jnp.dot(ya_ref[...], w_ref[:D_A, :], preferred_element_type=jnp.float32)
    y = y + jnp.dot(yb_ref[...], w_ref[D_A:, :], preferred_element_type=jnp.float32)
    o_ref[...] = x_ref[...] + y


def _out_projection(x2d, ya, yb, w_out_bf16, tm):
    m = x2d.shape[0]
    row = lambda i: (i, 0)
    return pl.pallas_call(
        _outproj_kernel,
        grid=(m // tm,),
        in_specs=[
            pl.BlockSpec((tm, D_MODEL), row),
            pl.BlockSpec((tm, D_A), row),
            pl.BlockSpec((tm, D_B), row),
            pl.BlockSpec((D_A + D_B, D_MODEL), lambda i: (0, 0)),
        ],
        out_specs=pl.BlockSpec((tm, D_MODEL), row),
        out_shape=jax.ShapeDtypeStruct((m, D_MODEL), jnp.float32),
        name="out_projection",
        compiler_params=pltpu.CompilerParams(
            dimension_semantics=("arbitrary",),
            vmem_limit_bytes=VMEM_LIMIT),
    )(x2d, ya, yb, w_out_bf16)


def _rope_tables(rows, cols):
    half = HEAD_DIM // 2
    inv_freq = ROPE_THETA ** (-jnp.arange(0, half, 2, dtype=jnp.float32) / half)
    ang_r = rows.astype(jnp.float32)[:, None] * inv_freq[None, :]
    ang_c = cols.astype(jnp.float32)[:, None] * inv_freq[None, :]
    n_r, n_c = rows.shape[0], cols.shape[0]

    def grid(r_part, c_part):
        r_part = jnp.broadcast_to(r_part[:, None, :], (n_r, n_c, r_part.shape[-1]))
        c_part = jnp.broadcast_to(c_part[None, :, :], (n_r, n_c, c_part.shape[-1]))
        return jnp.concatenate([r_part, c_part], axis=-1).reshape(n_r * n_c, HEAD_DIM)

    cr, sr, cc, sc = jnp.cos(ang_r), jnp.sin(ang_r), jnp.cos(ang_c), jnp.sin(ang_c)
    zr, zc = jnp.zeros_like(sr), jnp.zeros_like(sc)
    cos = grid(jnp.concatenate([cr, cr], -1), jnp.concatenate([cc, cc], -1))
    sin_a = grid(jnp.concatenate([-sr, zr], -1), jnp.concatenate([-sc, zc], -1))
    sin_b = grid(jnp.concatenate([zr, sr], -1), jnp.concatenate([zc, sc], -1))
    return cos, sin_a, sin_b


def _nbr_bias_tables(rpb):
    rows = 32
    tile_r = np.array([0, 2 * NB_QROWS, rows - NB_QROWS])[:, None, None]
    dr = np.arange(NB_QROWS)[None, :, None]
    jj = np.arange(NB_KROWS)[None, None, :]
    k0 = np.clip(tile_r - NB_QROWS, 0, rows - NB_KROWS)
    r = tile_r + dr
    kr = k0 + jj
    r0 = np.clip(r - WIN_R // 2, 0, rows - WIN_R)
    row_valid = (kr >= r0) & (kr < r0 + WIN_R)
    off_r = np.clip(kr - r + (WIN_R - 1), 0, 2 * WIN_R - 2)
    n_off_c = 2 * WIN_C - 1
    pos = np.arange(GRID_W)
    onehot = (pos[None, :, None] - pos[None, None, :] + (WIN_C - 1)
              == np.arange(n_off_c)[:, None, None]).astype(np.float32)
    toeplitz = jnp.einsum('hro,okc->hrkc', rpb.astype(jnp.float32) * LOG2E, onehot,
                          precision=lax.Precision.HIGHEST)

    def assemble(toep_ref, out_ref):
        kc = lax.broadcasted_iota(jnp.int32, (GRID_W, GRID_W), 0)
        c = lax.broadcasted_iota(jnp.int32, (GRID_W, GRID_W), 1)
        c0 = jnp.clip(c - WIN_C // 2, 0, GRID_W - WIN_C)
        col_valid = (kc >= c0) & (kc < c0 + WIN_C)
        masked = jnp.full((GRID_W, GRID_W), MASK_VALUE, jnp.float32)
        for cls in range(NB_CLASSES):
            for j in range(NB_KROWS):
                blocks = [jnp.where(col_valid, toep_ref[0, int(off_r[cls, d, j])], MASK_VALUE)
                          if row_valid[cls, d, j] else masked for d in range(NB_QROWS)]
                out_ref[0, cls, j * GRID_W:(j + 1) * GRID_W, :] = jnp.concatenate(blocks, axis=1)

    return pl.pallas_call(
        assemble,
        grid=(N_HEADS_B,),
        in_specs=[pl.BlockSpec((1, 2 * WIN_R - 1, GRID_W, GRID_W), lambda h: (h, 0, 0, 0))],
        out_specs=pl.BlockSpec((1, NB_CLASSES, NB_K, NB_Q), lambda h: (h, 0, 0, 0)),
        out_shape=jax.ShapeDtypeStruct((N_HEADS_B, NB_CLASSES, NB_K, NB_Q), jnp.float32),
        name="nbr_bias_table",
    )(toeplitz)


def kernel(x_prompt, x_sample, meta_tokens, norm_w, w_in, q_norm_a, k_norm_a, q_norm_b, k_norm_b, rpb, w_out):
    w_in_bf = w_in[0].astype(jnp.bfloat16)
    w_out_bf = w_out[0].astype(jnp.bfloat16)
    nw = norm_w[0].reshape(1, D_MODEL)
    gains = [g[0].reshape(1, HEAD_DIM) for g in (q_norm_a, k_norm_a, q_norm_b, k_norm_b)]
    bias = _nbr_bias_tables(rpb[0])

    meta_tabs = _rope_tables(jnp.full((1,), -1, jnp.int32), jnp.arange(META_PAD, dtype=jnp.int32))
    meta_x = jnp.pad(meta_tokens, ((0, META_PAD - N_META), (0, 0)))
    score_bound_a = (HEAD_DIM * Q_SCALE * jnp.max(jnp.abs(q_norm_a[0])) * jnp.max(jnp.abs(k_norm_a[0])))
    safe_a = (score_bound_a <= SAFE_SCORE_BOUND).astype(jnp.int32).reshape(1)
    score_bound_b = (HEAD_DIM * Q_SCALE * jnp.max(jnp.abs(q_norm_b[0])) * jnp.max(jnp.abs(k_norm_b[0]))
                     + LOG2E * jnp.max(jnp.abs(rpb[0])))
    safe_b = (score_bound_b <= SAFE_SCORE_BOUND).astype(jnp.int32).reshape(1)
    meta_out = _in_projection(meta_x, META_PAD, META_PAD, META_PAD, nw, w_in_bf, gains, meta_tabs)
    _, km_a, vmt_a, _, _, km_b, vmt_b, _ = meta_out

    max_seq = max(x_prompt.shape[1], x_sample.shape[1])
    tabs = _rope_tables(jnp.arange(max_seq // GRID_W, dtype=jnp.int32),
                        jnp.arange(GRID_W, dtype=jnp.int32))

    def encode(x):
        batch, seq, _ = x.shape
        x2d = x.reshape(batch * seq, D_MODEL)
        qat, ka, vat, ga, qbt, kb, vbt, gb = _in_projection(
            x2d, seq, IN_ROW_TILE, GQA_KEY_CHUNK, nw, w_in_bf, gains, tabs)
        ya = _global_attention(safe_a, qat, ka, vat, ga, km_a, vmt_a, batch, seq, GQA_Q_TILE)
        yb = _neighbourhood_attention(safe_b, qbt, kb, vbt, gb, km_b, vmt_b, bias, seq)
        y = _out_projection(x2d, ya, yb, w_out_bf, OUT_ROW_TILE)
        return y.reshape(batch, seq, D_MODEL)

    return (encode(x_prompt), encode(x_sample))
```

```python
import functools
import math

import jax
import jax.numpy as jnp
import numpy as np
from jax import lax
from jax.experimental import pallas as pl
from jax.experimental.pallas import tpu as pltpu

D_MODEL = 2048
HEAD_DIM = 128
N_HEADS_A = 8
N_KV_A = 2
GROUP_A = N_HEADS_A // N_KV_A
N_HEADS_B = 8
D_A = N_HEADS_A * HEAD_DIM
D_KV_A = N_KV_A * HEAD_DIM
D_B = N_HEADS_B * HEAD_DIM
D_IN = D_A + 2 * D_KV_A + D_A + 4 * D_B
N_META = 16
GRID_W = 64
WIN_R = 8
WIN_C = 16
ROPE_THETA = 10000.0
EPS = 1e-6

LOG2E = math.log2(math.e)
Q_SCALE = HEAD_DIM ** -0.5 * LOG2E
MASK_VALUE = -1e30

LANES = 128
BF16_SUBLANE_TILE = 16
VT_EXTRA_ROWS = BF16_SUBLANE_TILE
VT_ROWS = HEAD_DIM + VT_EXTRA_ROWS
SAFE_SCORE_BOUND = 60.0
META_PAD = LANES
VMEM_LIMIT = 56 * 1024 * 1024

TN = 512
N_COL_TILES = D_IN // TN
NB_QROWS = 4
NB_Q = NB_QROWS * GRID_W
NB_KROWS = 3 * NB_QROWS
NB_K = NB_KROWS * GRID_W
NB_CLASSES = 3

IN_ROW_TILE = 256
OUT_ROW_TILE = 512
GQA_KEY_CHUNK = 1024
GQA_Q_TILE = 512
NB_TILES_PER_STEP = 8
NB_VCHUNK = IN_ROW_TILE


def _silu(z):
    return z * (1.0 / (1.0 + jnp.exp(-z)))


def _head_norm(a, gain):
    ms = jnp.mean(a * a, axis=-1, keepdims=True)
    return a * lax.rsqrt(ms + EPS) * gain


def _rope(y, cos, sin_a, sin_b):
    return y * cos + pltpu.roll(y, 96, 1) * sin_a + pltpu.roll(y, 32, 1) * sin_b


def _inproj_kernel(x_ref, nw_ref, w_ref, gqa_ref, gka_ref, gqb_ref, gkb_ref,
                   cos_ref, sa_ref, sb_ref,
                   qat_ref, ka_ref, vat_ref, za_ref, qbt_ref, kb_ref, vbt_ref, zb_ref,
                   xn_ref):
    x = x_ref[...]
    ms = jnp.mean(x * x, axis=-1, keepdims=True)
    xn_ref[...] = (x * lax.rsqrt(ms + EPS) * nw_ref[...]).astype(jnp.bfloat16)

    def column_tile(j):
        return jnp.dot(xn_ref[...], w_ref[:, j * TN:(j + 1) * TN],
                       preferred_element_type=jnp.float32)

    def normed_roped(a, gain_ref):
        return _rope(_head_norm(a, gain_ref[...]), cos_ref[...], sa_ref[...], sb_ref[...])

    def store_heads(ref, first, vals):
        for h, v in enumerate(vals):
            ref[:, (first + h) * HEAD_DIM:(first + h + 1) * HEAD_DIM] = v.astype(ref.dtype)

    heads_per_tile = TN // HEAD_DIM

    def store_values_t(ref, first, vals):
        ones_row = (lax.broadcasted_iota(jnp.int32, (VT_EXTRA_ROWS, vals[0].shape[0]), 0) == 0)
        for h, a in enumerate(vals):
            ref[first + h, 0, :HEAD_DIM, :] = a.T.astype(ref.dtype)
            ref[first + h, 0, HEAD_DIM:, :] = ones_row.astype(ref.dtype)

    def epilogue(j, acc):
        heads = [acc[:, h * HEAD_DIM:(h + 1) * HEAD_DIM] for h in range(heads_per_tile)]
        if j < 2:
            for h, a in enumerate(heads):
                q = normed_roped(a, gqa_ref) * Q_SCALE
                qat_ref[j * heads_per_tile + h] = q.T.astype(qat_ref.dtype)
        elif j == 2:
            store_heads(ka_ref, 0, [normed_roped(a, gka_ref) for a in heads[:N_KV_A]])
            store_values_t(vat_ref, 0, heads[N_KV_A:])
        elif j < 5:
            za_ref[:, (j - 3) * TN:(j - 2) * TN] = _silu(acc)
        elif j < 7:
            for h, a in enumerate(heads):
                q = _head_norm(a, gqb_ref[...]) * Q_SCALE
                qbt_ref[(j - 5) * heads_per_tile + h] = q.T.astype(qbt_ref.dtype)
        elif j < 9:
            store_heads(kb_ref, (j - 7) * heads_per_tile,
                        [_head_norm(a, gkb_ref[...]) for a in heads])
        elif j < 11:
            store_values_t(vbt_ref, (j - 9) * heads_per_tile, heads)
        else:
            zb_ref[:, (j - 11) * TN:(j - 10) * TN] = _silu(acc)

    acc = column_tile(0)
    for j in range(N_COL_TILES):
        nxt = column_tile(j + 1) if j + 1 < N_COL_TILES else None
        epilogue(j, acc)
        acc = nxt


def _in_projection(x2d, seq, tm, chunk, norm_w, w_in_bf16, gains, tabs):
    m = x2d.shape[0]
    n_tab = seq // tm
    sub = chunk // tm
    cos, sin_a, sin_b = tabs
    row = lambda i: (i, 0)
    const = lambda i: (0, 0)
    bf, f32 = jnp.bfloat16, jnp.float32
    out_shape = [
        jax.ShapeDtypeStruct((N_HEADS_A, HEAD_DIM, m), bf), jax.ShapeDtypeStruct((m, D_KV_A), bf),
        jax.ShapeDtypeStruct((N_KV_A, m // chunk, VT_ROWS, chunk), bf),
        jax.ShapeDtypeStruct((m, D_A), f32),
        jax.ShapeDtypeStruct((N_HEADS_B, HEAD_DIM, m), bf), jax.ShapeDtypeStruct((m, D_B), bf),
        jax.ShapeDtypeStruct((N_HEADS_B, m // tm, VT_ROWS, tm), bf),
        jax.ShapeDtypeStruct((m, D_B), f32),
    ]
    out_specs = [
        pl.BlockSpec((N_HEADS_A, HEAD_DIM, tm), lambda i: (0, 0, i)),
        pl.BlockSpec((tm, D_KV_A), row),
        pl.BlockSpec((N_KV_A, 1, VT_ROWS, tm), lambda i: (0, i // sub, 0, i % sub)),
        pl.BlockSpec((tm, D_A), row),
        pl.BlockSpec((N_HEADS_B, HEAD_DIM, tm), lambda i: (0, 0, i)),
        pl.BlockSpec((tm, D_B), row),
        pl.BlockSpec((N_HEADS_B, 1, VT_ROWS, tm), lambda i: (0, i, 0, 0)),
        pl.BlockSpec((tm, D_B), row),
    ]
    gain_spec = pl.BlockSpec((1, HEAD_DIM), const)
    tab_spec = pl.BlockSpec((tm, HEAD_DIM), lambda i: (i % n_tab, 0))
    return pl.pallas_call(
        _inproj_kernel,
        grid=(m // tm,),
        in_specs=[
            pl.BlockSpec((tm, D_MODEL), row),
            pl.BlockSpec((1, D_MODEL), const),
            pl.BlockSpec((D_MODEL, D_IN), const, pipeline_mode=pl.Buffered(1)),
            gain_spec, gain_spec, gain_spec, gain_spec,
            tab_spec, tab_spec, tab_spec,
        ],
        out_specs=out_specs,
        out_shape=out_shape,
        scratch_shapes=[pltpu.VMEM((tm, D_MODEL), bf)],
        name="in_projection",
        compiler_params=pltpu.CompilerParams(
            dimension_semantics=("arbitrary",),
            vmem_limit_bytes=VMEM_LIMIT),
    )(x2d, norm_w, w_in_bf16, *gains, cos, sin_a, sin_b)


def _nt_dot(a, b):
    return lax.dot_general(a, b, (((1,), (1,)), ((), ())), preferred_element_type=jnp.float32)


GQA_SLAB_WIDTH = 512


def _gqa_kernel(safe_ref, qt_ref, k_ref, vt_ref, km_ref, vmt_ref, g_ref, o_ref, acc_ref, s_ref):
    tq = qt_ref.shape[2]
    n_chunks, chunk = vt_ref.shape[1], vt_ref.shape[3]
    width = acc_ref.shape[2]
    n_slabs = acc_ref.shape[0]
    heads_per_slab = width // tq
    km = km_ref[:N_META, :]
    vmt = vmt_ref[0, 0]
    meta_pad = jnp.zeros((META_PAD - N_META, width), jnp.bfloat16)

    def meta_values(p):
        return dot(vmt, jnp.concatenate([p.astype(jnp.bfloat16), meta_pad], axis=0))
    qts = [jnp.concatenate([qt_ref[i * heads_per_slab + j] for j in range(heads_per_slab)], axis=1)
           if heads_per_slab > 1 else qt_ref[i] for i in range(n_slabs)]

    def dot(a, b):
        return jnp.dot(a, b, preferred_element_type=jnp.float32)

    def keys(c):
        if isinstance(c, int):
            return k_ref[c * chunk:(c + 1) * chunk, :]
        return k_ref[pl.ds(pl.multiple_of(c * chunk, chunk), chunk), :]

    def attend(init, consume):
        def chunk_stages(c, state, has_next):
            state = list(state)
            k = keys(c)
            vt = vt_ref[0, c]
            for i in range(n_slabs):
                if i + 1 < n_slabs:
                    s_ref[(i + 1) % 2] = dot(k, qts[i + 1])
                elif has_next:
                    s_ref[0] = dot(keys(c + 1), qts[0])
                state[i] = consume(i, s_ref[i % 2], vt, state[i])
            return tuple(state)

        s_ref[0] = dot(keys(0), qts[0])
        s_meta = [dot(km, qt) for qt in qts]
        state = tuple(init(i, s) for i, s in enumerate(s_meta))
        state = chunk_stages(0, state, n_chunks > 1)
        if n_chunks > 1:
            state = lax.fori_loop(1, n_chunks - 1, lambda c, st: chunk_stages(c, st, True), state)
            chunk_stages(n_chunks - 1, state, False)
        for i in range(n_slabs):
            acc = acc_ref[i]
            o_slab = acc[:HEAD_DIM] * (1.0 / acc[HEAD_DIM:HEAD_DIM + 1])
            for j in range(heads_per_slab):
                h = i * heads_per_slab + j
                sl = slice(h * HEAD_DIM, (h + 1) * HEAD_DIM)
                o = o_slab[:, j * tq:(j + 1) * tq].T
                o_ref[:, sl] = (o * g_ref[:, sl]).astype(o_ref.dtype)

    safe = safe_ref[0] != 0

    @pl.when(safe)
    def _():
        def init(i, s):
            acc_ref[i] = meta_values(jnp.exp2(s))
            return 0

        def consume(i, s, vt, state):
            acc_ref[i] += dot(vt, jnp.exp2(s).astype(jnp.bfloat16))
            return state

        attend(init, consume)

    @pl.when(jnp.logical_not(safe))
    def _():
        def init(i, s):
            m = jnp.max(s, axis=0, keepdims=True)
            acc_ref[i] = meta_values(jnp.exp2(s - m))
            return m

        def consume(i, s, vt, m):
            m_new = jnp.maximum(m, jnp.max(s, axis=0, keepdims=True))
            p = jnp.exp2(s - m_new)
            acc_ref[i] = jnp.exp2(m - m_new) * acc_ref[i] + dot(vt, p.astype(jnp.bfloat16))
            return m_new

        attend(init, consume)


def _global_attention(safe, qat, ka, vat, gate, km, vmt, batch, seq, tq):
    m = gate.shape[0]
    nq = seq // tq
    chunk = vat.shape[3]
    n_chunks = seq // chunk
    qmap = lambda b, g, i: (b * nq + i, g)
    width = GROUP_A * HEAD_DIM
    return pl.pallas_call(
        _gqa_kernel,
        grid=(batch, N_KV_A, nq),
        in_specs=[
            pl.BlockSpec(memory_space=pltpu.SMEM),
            pl.BlockSpec((GROUP_A, HEAD_DIM, tq), lambda b, g, i: (g, 0, b * nq + i)),
            pl.BlockSpec((seq, HEAD_DIM), lambda b, g, i: (b, g)),
            pl.BlockSpec((1, n_chunks, VT_ROWS, chunk), lambda b, g, i: (g, b, 0, 0)),
            pl.BlockSpec((META_PAD, HEAD_DIM), lambda b, g, i: (0, g)),
            pl.BlockSpec((1, 1, VT_ROWS, META_PAD), lambda b, g, i: (g, 0, 0, 0)),
            pl.BlockSpec((tq, width), qmap),
        ],
        out_specs=pl.BlockSpec((tq, width), qmap),
        out_shape=jax.ShapeDtypeStruct((m, D_A), jnp.bfloat16),
        scratch_shapes=[pltpu.VMEM((GROUP_A * tq // GQA_SLAB_WIDTH, VT_ROWS, GQA_SLAB_WIDTH), jnp.float32),
                        pltpu.VMEM((2, chunk, GQA_SLAB_WIDTH), jnp.float32)],
        name="global_attention",
        compiler_params=pltpu.CompilerParams(
            dimension_semantics=("arbitrary", "arbitrary", "arbitrary"),
            vmem_limit_bytes=VMEM_LIMIT),
    )(safe, qat, ka, vat, km, vmt, gate)


def _nbr_kernel(safe_ref, qt_ref, k_ref, vt_ref, km_ref, vmt_ref, bias_ref, g_ref, o_ref, *, n_tiles):
    step = pl.program_id(2)
    bf = jnp.bfloat16
    km = km_ref[:N_META, :]
    vmt = vmt_ref[0, 0]
    meta_pad = jnp.zeros((META_PAD - N_META, NB_Q), bf)
    chunks_per_window = NB_K // NB_VCHUNK

    def dot(a, b):
        return jnp.dot(a, b, preferred_element_type=jnp.float32)

    def run(probs_fn):
        tiles = []
        for u in range(NB_TILES_PER_STEP):
            t = step * NB_TILES_PER_STEP + u
            cls = jnp.where(t == 0, 0, jnp.where(t == n_tiles - 1, 2, 1))
            first_chunk = jnp.clip(t - 1, 0, n_tiles - chunks_per_window)
            tiles.append((u, cls, first_chunk))
        scores = []
        for u, cls, first_chunk in tiles:
            qt = qt_ref[0, :, u * NB_Q:(u + 1) * NB_Q]
            kstart = pl.multiple_of(first_chunk * NB_VCHUNK, NB_VCHUNK)
            s_win = dot(k_ref[pl.ds(kstart, NB_K), :], qt) + bias_ref[0, cls]
            scores.append((s_win, dot(km, qt)))
        probs = [probs_fn(s_win, s_meta) for s_win, s_meta in scores]
        for (u, cls, first_chunk), (p_win, p_meta) in zip(tiles, probs):
            acc = dot(vmt, jnp.concatenate([p_meta, meta_pad], axis=0))
            for j in range(chunks_per_window):
                acc += dot(vt_ref[0, first_chunk + j], p_win[j * NB_VCHUNK:(j + 1) * NB_VCHUNK])
            o = (acc[:HEAD_DIM] * (1.0 / acc[HEAD_DIM:HEAD_DIM + 1])).T
            rows = slice(u * NB_Q, (u + 1) * NB_Q)
            o_ref[rows, :] = (o * g_ref[rows, :]).astype(o_ref.dtype)

    def probs_safe(s_win, s_meta):
        return jnp.exp2(s_win).astype(bf), jnp.exp2(s_meta).astype(bf)

    def probs_max(s_win, s_meta):
        m = jnp.maximum(jnp.max(s_win, axis=0, keepdims=True), jnp.max(s_meta, axis=0, keepdims=True))
        return jnp.exp2(s_win - m).astype(bf), jnp.exp2(s_meta - m).astype(bf)

    safe = safe_ref[0] != 0

    @pl.when(safe)
    def _():
        run(probs_safe)

    @pl.when(jnp.logical_not(safe))
    def _():
        run(probs_max)


def _neighbourhood_attention(safe, qbt, kb, vbt, gate, km, vmt, bias, batch, seq):
    m = gate.shape[0]
    tqb = NB_TILES_PER_STEP * NB_Q
    nq = seq // tqb
    n_vchunks = seq // NB_VCHUNK
    assert vbt.shape[3] == NB_VCHUNK
    qmap = lambda h, b, i: (b * nq + i, h)
    return pl.pallas_call(
        functools.partial(_nbr_kernel, n_tiles=seq // NB_Q),
        grid=(N_HEADS_B, batch, nq),
        in_specs=[
            pl.BlockSpec(memory_space=pltpu.SMEM),
            pl.BlockSpec((1, HEAD_DIM, tqb), lambda h, b, i: (h, 0, b * nq + i)),
            pl.BlockSpec((seq, HEAD_DIM), lambda h, b, i: (b, h)),
            pl.BlockSpec((1, n_vchunks, VT_ROWS, NB_VCHUNK), lambda h, b, i: (h, b, 0, 0)),
            pl.BlockSpec((META_PAD, HEAD_DIM), lambda h, b, i: (0, h)),
            pl.BlockSpec((1, 1, VT_ROWS, META_PAD), lambda h, b, i: (h, 0, 0, 0)),
            pl.BlockSpec((1, NB_CLASSES, NB_K, NB_Q), lambda h, b, i: (h, 0, 0, 0)),
            pl.BlockSpec((tqb, HEAD_DIM), qmap),
        ],
        out_specs=pl.BlockSpec((tqb, HEAD_DIM), qmap),
        out_shape=jax.ShapeDtypeStruct((m, D_B), jnp.bfloat16),
        name="neighbourhood_attention",
        compiler_params=pltpu.CompilerParams(
            dimension_semantics=("arbitrary", "arbitrary", "arbitrary"),
            vmem_limit_bytes=VMEM_LIMIT),
    )(safe, qbt, kb, vbt, km, vmt, bias, gate)


def _outproj_kernel(x_ref, ya_ref, yb_ref, w_ref, o_ref):
    y = jnp.dot(ya_ref[...], w_ref[:D_A, :], preferred_element_type=jnp.float32)
    y = y + jnp.dot(yb_ref[...], w_ref[D_A:, :], preferred_element_type=jnp.float32)
    o_ref[...] = x_ref[...] + y


def _out_projection(x2d, ya, yb, w_out_bf16, tm):
    m = x2d.shape[0]
    row = lambda i: (i, 0)
    return pl.pallas_call(
        _outproj_kernel,
        grid=(m // tm,),
        in_specs=[
            pl.BlockSpec((tm, D_MODEL), row),
            pl.BlockSpec((tm, D_A), row),
            pl.BlockSpec((tm, D_B), row),
            pl.BlockSpec((D_A + D_B, D_MODEL), lambda i: (0, 0)),
        ],
        out_specs=pl.BlockSpec((tm, D_MODEL), row),
        out_shape=jax.ShapeDtypeStruct((m, D_MODEL), jnp.float32),
        name="out_projection",
        compiler_params=pltpu.CompilerParams(
            dimension_semantics=("arbitrary",),
            vmem_limit_bytes=VMEM_LIMIT),
    )(x2d, ya, yb, w_out_bf16)


def _rope_tables(rows, cols):
    half = HEAD_DIM // 2
    inv_freq = ROPE_THETA ** (-jnp.arange(0, half, 2, dtype=jnp.float32) / half)
    ang_r = rows.astype(jnp.float32)[:, None] * inv_freq[None, :]
    ang_c = cols.astype(jnp.float32)[:, None] * inv_freq[None, :]
    n_r, n_c = rows.shape[0], cols.shape[0]

    def grid(r_part, c_part):
        r_part = jnp.broadcast_to(r_part[:, None, :], (n_r, n_c, r_part.shape[-1]))
        c_part = jnp.broadcast_to(c_part[None, :, :], (n_r, n_c, c_part.shape[-1]))
        return jnp.concatenate([r_part, c_part], axis=-1).reshape(n_r * n_c, HEAD_DIM)

    cr, sr, cc, sc = jnp.cos(ang_r), jnp.sin(ang_r), jnp.cos(ang_c), jnp.sin(ang_c)
    zr, zc = jnp.zeros_like(sr), jnp.zeros_like(sc)
    cos = grid(jnp.concatenate([cr, cr], -1), jnp.concatenate([cc, cc], -1))
    sin_a = grid(jnp.concatenate([-sr, zr], -1), jnp.concatenate([-sc, zc], -1))
    sin_b = grid(jnp.concatenate([zr, sr], -1), jnp.concatenate([zc, sc], -1))
    return cos, sin_a, sin_b


def _nbr_bias_tables(rpb):
    rows = 32
    tile_r = np.array([0, 2 * NB_QROWS, rows - NB_QROWS])[:, None, None]
    dr = np.arange(NB_QROWS)[None, :, None]
    jj = np.arange(NB_KROWS)[None, None, :]
    k0 = np.clip(tile_r - NB_QROWS, 0, rows - NB_KROWS)
    r = tile_r + dr
    kr = k0 + jj
    r0 = np.clip(r - WIN_R // 2, 0, rows - WIN_R)
    row_valid = (kr >= r0) & (kr < r0 + WIN_R)
    off_r = np.clip(kr - r + (WIN_R - 1), 0, 2 * WIN_R - 2)
    n_off_c = 2 * WIN_C - 1
    pos = np.arange(GRID_W)
    onehot = (pos[None, :, None] - pos[None, None, :] + (WIN_C - 1)
              == np.arange(n_off_c)[:, None, None]).astype(np.float32)
    toeplitz = jnp.einsum('hro,okc->hrkc', rpb.astype(jnp.float32) * LOG2E, onehot,
                          precision=lax.Precision.HIGHEST)

    def assemble(toep_ref, out_ref):
        kc = lax.broadcasted_iota(jnp.int32, (GRID_W, GRID_W), 0)
        c = lax.broadcasted_iota(jnp.int32, (GRID_W, GRID_W), 1)
        c0 = jnp.clip(c - WIN_C // 2, 0, GRID_W - WIN_C)
        col_valid = (kc >= c0) & (kc < c0 + WIN_C)
        masked = jnp.full((GRID_W, GRID_W), MASK_VALUE, jnp.float32)
        for cls in range(NB_CLASSES):
            for j in range(NB_KROWS):
                blocks = [jnp.where(col_valid, toep_ref[0, int(off_r[cls, d, j])], MASK_VALUE)
                          if row_valid[cls, d, j] else masked for d in range(NB_QROWS)]
                out_ref[0, cls, j * GRID_W:(j + 1) * GRID_W, :] = jnp.concatenate(blocks, axis=1)

    return pl.pallas_call(
        assemble,
        grid=(N_HEADS_B,),
        in_specs=[pl.BlockSpec((1, 2 * WIN_R - 1, GRID_W, GRID_W), lambda h: (h, 0, 0, 0))],
        out_specs=pl.BlockSpec((1, NB_CLASSES, NB_K, NB_Q), lambda h: (h, 0, 0, 0)),
        out_shape=jax.ShapeDtypeStruct((N_HEADS_B, NB_CLASSES, NB_K, NB_Q), jnp.float32),
        name="nbr_bias_table",
    )(toeplitz)


def kernel(x_prompt, x_sample, meta_tokens, norm_w, w_in, q_norm_a, k_norm_a, q_norm_b, k_norm_b, rpb, w_out):
    w_in_bf = w_in[0].astype(jnp.bfloat16)
    w_out_bf = w_out[0].astype(jnp.bfloat16)
    nw = norm_w[0].reshape(1, D_MODEL)
    gains = [g[0].reshape(1, HEAD_DIM) for g in (q_norm_a, k_norm_a, q_norm_b, k_norm_b)]
    bias = _nbr_bias_tables(rpb[0])

    meta_tabs = _rope_tables(jnp.full((1,), -1, jnp.int32), jnp.arange(META_PAD, dtype=jnp.int32))
    meta_x = jnp.pad(meta_tokens, ((0, META_PAD - N_META), (0, 0)))
    score_bound_a = (HEAD_DIM * Q_SCALE * jnp.max(jnp.abs(q_norm_a[0])) * jnp.max(jnp.abs(k_norm_a[0])))
    safe_a = (score_bound_a <= SAFE_SCORE_BOUND).astype(jnp.int32).reshape(1)
    score_bound_b = (HEAD_DIM * Q_SCALE * jnp.max(jnp.abs(q_norm_b[0])) * jnp.max(jnp.abs(k_norm_b[0]))
                     + LOG2E * jnp.max(jnp.abs(rpb[0])))
    safe_b = (score_bound_b <= SAFE_SCORE_BOUND).astype(jnp.int32).reshape(1)
    meta_out = _in_projection(meta_x, META_PAD, META_PAD, META_PAD, nw, w_in_bf, gains, meta_tabs)
    _, km_a, vmt_a, _, _, km_b, vmt_b, _ = meta_out

    max_seq = max(x_prompt.shape[1], x_sample.shape[1])
    tabs = _rope_tables(jnp.arange(max_seq // GRID_W, dtype=jnp.int32),
                        jnp.arange(GRID_W, dtype=jnp.int32))

    def encode(x):
        batch, seq, _ = x.shape
        x2d = x.reshape(batch * seq, D_MODEL)
        qat, ka, vat, ga, qbt, kb, vbt, gb = _in_projection(
            x2d, seq, IN_ROW_TILE, GQA_KEY_CHUNK, nw, w_in_bf, gains, tabs)
        ya = _global_attention(safe_a, qat, ka, vat, ga, km_a, vmt_a, batch, seq, GQA_Q_TILE)
        yb = _neighbourhood_attention(safe_b, qbt, kb, vbt, gb, km_b, vmt_b, bias, batch, seq)
        y = _out_projection(x2d, ya, yb, w_out_bf, OUT_ROW_TILE)
        return y.reshape(batch, seq, D_MODEL)

    return (encode(x_prompt), encode(x_sample))
```

```python
import functools
import math

import jax
import jax.numpy as jnp
import numpy as np
from jax import lax
from jax.experimental import pallas as pl
from jax.experimental.pallas import tpu as pltpu

D_MODEL = 2048
HEAD_DIM = 128
N_HEADS_A = 8
N_KV_A = 2
GROUP_A = N_HEADS_A // N_KV_A
N_HEADS_B = 8
D_A = N_HEADS_A * HEAD_DIM
D_KV_A = N_KV_A * HEAD_DIM
D_B = N_HEADS_B * HEAD_DIM
D_IN = D_A + 2 * D_KV_A + D_A + 4 * D_B
N_META = 16
GRID_W = 64
WIN_R = 8
WIN_C = 16
ROPE_THETA = 10000.0
EPS = 1e-6

LOG2E = math.log2(math.e)
Q_SCALE = HEAD_DIM ** -0.5 * LOG2E
MASK_VALUE = -1e30

LANES = 128
BF16_SUBLANE_TILE = 16
VT_EXTRA_ROWS = BF16_SUBLANE_TILE
VT_ROWS = HEAD_DIM + VT_EXTRA_ROWS
SAFE_SCORE_BOUND = 60.0
META_PAD = LANES
VMEM_LIMIT = 56 * 1024 * 1024

TN = 512
N_COL_TILES = D_IN // TN
NB_QROWS = 4
NB_Q = NB_QROWS * GRID_W
NB_KROWS = 3 * NB_QROWS
NB_K = NB_KROWS * GRID_W
NB_CLASSES = 3

IN_ROW_TILE = 256
OUT_ROW_TILE = 512
GQA_KEY_CHUNK = 1024
GQA_Q_TILE = 512
NB_TILES_PER_STEP = 8
NB_VCHUNK = IN_ROW_TILE


def _silu(z):
    return z * (1.0 / (1.0 + jnp.exp(-z)))


def _head_norm(a, gain):
    ms = jnp.mean(a * a, axis=-1, keepdims=True)
    return a * lax.rsqrt(ms + EPS) * gain


def _rope(y, cos, sin_a, sin_b):
    return y * cos + pltpu.roll(y, 96, 1) * sin_a + pltpu.roll(y, 32, 1) * sin_b


def _inproj_kernel(x_ref, nw_ref, w_ref, gqa_ref, gka_ref, gqb_ref, gkb_ref,
                   cos_ref, sa_ref, sb_ref,
                   qat_ref, ka_ref, vat_ref, za_ref, qbt_ref, kb_ref, vbt_ref, zb_ref,
                   xn_ref):
    x = x_ref[...]
    ms = jnp.mean(x * x, axis=-1, keepdims=True)
    xn_ref[...] = (x * lax.rsqrt(ms + EPS) * nw_ref[...]).astype(jnp.bfloat16)

    def column_tile(j):
        return jnp.dot(xn_ref[...], w_ref[:, j * TN:(j + 1) * TN],
                       preferred_element_type=jnp.float32)

    def normed_roped(a, gain_ref):
        return _rope(_head_norm(a, gain_ref[...]), cos_ref[...], sa_ref[...], sb_ref[...])

    def store_heads(ref, first, vals):
        for h, v in enumerate(vals):
            ref[first + h] = v.astype(ref.dtype)

    heads_per_tile = TN // HEAD_DIM

    def store_values_t(ref, first, vals):
        ones_row = (lax.broadcasted_iota(jnp.int32, (VT_EXTRA_ROWS, vals[0].shape[0]), 0) == 0)
        for h, a in enumerate(vals):
            ref[first + h, 0, :HEAD_DIM, :] = a.T.astype(ref.dtype)
            ref[first + h, 0, HEAD_DIM:, :] = ones_row.astype(ref.dtype)

    def epilogue(j, acc):
        heads = [acc[:, h * HEAD_DIM:(h + 1) * HEAD_DIM] for h in range(heads_per_tile)]
        if j < 2:
            for h, a in enumerate(heads):
                q = normed_roped(a, gqa_ref) * Q_SCALE
                qat_ref[j * heads_per_tile + h] = q.T.astype(qat_ref.dtype)
        elif j == 2:
            store_heads(ka_ref, 0, [normed_roped(a, gka_ref) for a in heads[:N_KV_A]])
            store_values_t(vat_ref, 0, heads[N_KV_A:])
        elif j < 5:
            za_ref[:, (j - 3) * TN:(j - 2) * TN] = _silu(acc)
        elif j < 7:
            for h, a in enumerate(heads):
                q = _head_norm(a, gqb_ref[...]) * Q_SCALE
                qbt_ref[(j - 5) * heads_per_tile + h] = q.T.astype(qbt_ref.dtype)
        elif j < 9:
            store_heads(kb_ref, (j - 7) * heads_per_tile,
                        [_head_norm(a, gkb_ref[...]) for a in heads])
        elif j < 11:
            store_values_t(vbt_ref, (j - 9) * heads_per_tile, heads)
        else:
            store_heads(zb_ref, (j - 11) * heads_per_tile, [_silu(a) for a in heads])

    acc = column_tile(0)
    for j in range(N_COL_TILES):
        nxt = column_tile(j + 1) if j + 1 < N_COL_TILES else None
        epilogue(j, acc)
        acc = nxt


def _in_projection(x2d, seq, tm, chunk, norm_w, w_in_bf16, gains, tabs):
    m = x2d.shape[0]
    n_tab = seq // tm
    sub = chunk // tm
    cos, sin_a, sin_b = tabs
    row = lambda i: (i, 0)
    const = lambda i: (0, 0)
    bf, f32 = jnp.bfloat16, jnp.float32
    head_rows = lambda i: (0, i, 0)
    out_shape = [
        jax.ShapeDtypeStruct((N_HEADS_A, HEAD_DIM, m), bf),
        jax.ShapeDtypeStruct((N_KV_A, m, HEAD_DIM), bf),
        jax.ShapeDtypeStruct((N_KV_A, m // chunk, VT_ROWS, chunk), bf),
        jax.ShapeDtypeStruct((m, D_A), f32),
        jax.ShapeDtypeStruct((N_HEADS_B, HEAD_DIM, m), bf),
        jax.ShapeDtypeStruct((N_HEADS_B, m, HEAD_DIM), bf),
        jax.ShapeDtypeStruct((N_HEADS_B, m // tm, VT_ROWS, tm), bf),
        jax.ShapeDtypeStruct((N_HEADS_B, m, HEAD_DIM), f32),
    ]
    out_specs = [
        pl.BlockSpec((N_HEADS_A, HEAD_DIM, tm), lambda i: (0, 0, i)),
        pl.BlockSpec((N_KV_A, tm, HEAD_DIM), head_rows),
        pl.BlockSpec((N_KV_A, 1, VT_ROWS, tm), lambda i: (0, i // sub, 0, i % sub)),
        pl.BlockSpec((tm, D_A), row),
        pl.BlockSpec((N_HEADS_B, HEAD_DIM, tm), lambda i: (0, 0, i)),
        pl.BlockSpec((N_HEADS_B, tm, HEAD_DIM), head_rows),
        pl.BlockSpec((N_HEADS_B, 1, VT_ROWS, tm), lambda i: (0, i, 0, 0)),
        pl.BlockSpec((N_HEADS_B, tm, HEAD_DIM), head_rows),
    ]
    gain_spec = pl.BlockSpec((1, HEAD_DIM), const)
    tab_spec = pl.BlockSpec((tm, HEAD_DIM), lambda i: (i % n_tab, 0))
    return pl.pallas_call(
        _inproj_kernel,
        grid=(m // tm,),
        in_specs=[
            pl.BlockSpec((tm, D_MODEL), row),
            pl.BlockSpec((1, D_MODEL), const),
            pl.BlockSpec((D_MODEL, D_IN), const, pipeline_mode=pl.Buffered(1)),
            gain_spec, gain_spec, gain_spec, gain_spec,
            tab_spec, tab_spec, tab_spec,
        ],
        out_specs=out_specs,
        out_shape=out_shape,
        scratch_shapes=[pltpu.VMEM((tm, D_MODEL), bf)],
        name="in_projection",
        compiler_params=pltpu.CompilerParams(
            dimension_semantics=("arbitrary",),
            vmem_limit_bytes=VMEM_LIMIT),
    )(x2d, norm_w, w_in_bf16, *gains, cos, sin_a, sin_b)


def _nt_dot(a, b):
    return lax.dot_general(a, b, (((1,), (1,)), ((), ())), preferred_element_type=jnp.float32)


GQA_SLAB_WIDTH = 512


def _gqa_kernel(safe_ref, qt_ref, k_ref, vt_ref, km_ref, vmt_ref, g_ref, o_ref, acc_ref, s_ref):
    tq = qt_ref.shape[2]
    n_chunks, chunk = vt_ref.shape[1], vt_ref.shape[3]
    width = acc_ref.shape[2]
    n_slabs = acc_ref.shape[0]
    heads_per_slab = width // tq
    km = km_ref[:N_META, :]
    vmt = vmt_ref[0, 0]
    meta_pad = jnp.zeros((META_PAD - N_META, width), jnp.bfloat16)

    def meta_values(p):
        return dot(vmt, jnp.concatenate([p.astype(jnp.bfloat16), meta_pad], axis=0))
    qts = [jnp.concatenate([qt_ref[i * heads_per_slab + j] for j in range(heads_per_slab)], axis=1)
           if heads_per_slab > 1 else qt_ref[i] for i in range(n_slabs)]

    def dot(a, b):
        return jnp.dot(a, b, preferred_element_type=jnp.float32)

    def keys(c):
        if isinstance(c, int):
            return k_ref[c * chunk:(c + 1) * chunk, :]
        return k_ref[pl.ds(pl.multiple_of(c * chunk, chunk), chunk), :]

    def attend(init, consume):
        def chunk_stages(c, state, has_next):
            state = list(state)
            k = keys(c)
            vt = vt_ref[0, c]
            for i in range(n_slabs):
                if i + 1 < n_slabs:
                    s_ref[(i + 1) % 2] = dot(k, qts[i + 1])
                elif has_next:
                    s_ref[0] = dot(keys(c + 1), qts[0])
                state[i] = consume(i, s_ref[i % 2], vt, state[i])
            return tuple(state)

        s_ref[0] = dot(keys(0), qts[0])
        s_meta = [dot(km, qt) for qt in qts]
        state = tuple(init(i, s) for i, s in enumerate(s_meta))
        state = chunk_stages(0, state, n_chunks > 1)
        if n_chunks > 1:
            state = lax.fori_loop(1, n_chunks - 1, lambda c, st: chunk_stages(c, st, True), state)
            chunk_stages(n_chunks - 1, state, False)
        for i in range(n_slabs):
            acc = acc_ref[i]
            o_slab = acc[:HEAD_DIM] * (1.0 / acc[HEAD_DIM:HEAD_DIM + 1])
            for j in range(heads_per_slab):
                h = i * heads_per_slab + j
                sl = slice(h * HEAD_DIM, (h + 1) * HEAD_DIM)
                o = o_slab[:, j * tq:(j + 1) * tq].T
                o_ref[:, sl] = (o * g_ref[:, sl]).astype(o_ref.dtype)

    safe = safe_ref[0] != 0

    @pl.when(safe)
    def _():
        def init(i, s):
            acc_ref[i] = meta_values(jnp.exp2(s))
            return 0

        def consume(i, s, vt, state):
            acc_ref[i] += dot(vt, jnp.exp2(s).astype(jnp.bfloat16))
            return state

        attend(init, consume)

    @pl.when(jnp.logical_not(safe))
    def _():
        def init(i, s):
            m = jnp.max(s, axis=0, keepdims=True)
            acc_ref[i] = meta_values(jnp.exp2(s - m))
            return m

        def consume(i, s, vt, m):
            m_new = jnp.maximum(m, jnp.max(s, axis=0, keepdims=True))
            p = jnp.exp2(s - m_new)
            acc_ref[i] = jnp.exp2(m - m_new) * acc_ref[i] + dot(vt, p.astype(jnp.bfloat16))
            return m_new

        attend(init, consume)


def _global_attention(safe, qat, ka, vat, gate, km, vmt, batch, seq, tq):
    m = gate.shape[0]
    nq = seq // tq
    chunk = vat.shape[3]
    n_chunks = seq // chunk
    qmap = lambda b, g, i: (b * nq + i, g)
    width = GROUP_A * HEAD_DIM
    return pl.pallas_call(
        _gqa_kernel,
        grid=(batch, N_KV_A, nq),
        in_specs=[
            pl.BlockSpec(memory_space=pltpu.SMEM),
            pl.BlockSpec((GROUP_A, HEAD_DIM, tq), lambda b, g, i: (g, 0, b * nq + i)),
            pl.BlockSpec((None, seq, HEAD_DIM), lambda b, g, i: (g, b, 0)),
            pl.BlockSpec((1, n_chunks, VT_ROWS, chunk), lambda b, g, i: (g, b, 0, 0)),
            pl.BlockSpec((None, META_PAD, HEAD_DIM), lambda b, g, i: (g, 0, 0)),
            pl.BlockSpec((1, 1, VT_ROWS, META_PAD), lambda b, g, i: (g, 0, 0, 0)),
            pl.BlockSpec((tq, width), qmap),
        ],
        out_specs=pl.BlockSpec((tq, width), qmap),
        out_shape=jax.ShapeDtypeStruct((m, D_A), jnp.bfloat16),
        scratch_shapes=[pltpu.VMEM((GROUP_A * tq // GQA_SLAB_WIDTH, VT_ROWS, GQA_SLAB_WIDTH), jnp.float32),
                        pltpu.VMEM((2, chunk, GQA_SLAB_WIDTH), jnp.float32)],
        name="global_attention",
        compiler_params=pltpu.CompilerParams(
            dimension_semantics=("arbitrary", "arbitrary", "arbitrary"),
            vmem_limit_bytes=VMEM_LIMIT),
    )(safe, qat, ka, vat, km, vmt, gate)


def _nbr_kernel(safe_ref, qt_ref, k_ref, vt_ref, km_ref, vmt_ref, bias_ref, g_ref, o_ref, *, n_tiles):
    step = pl.program_id(2)
    bf = jnp.bfloat16
    km = km_ref[:N_META, :]
    vmt = vmt_ref[0, 0]
    meta_pad = jnp.zeros((META_PAD - N_META, NB_Q), bf)
    chunks_per_window = NB_K // NB_VCHUNK

    def dot(a, b):
        return jnp.dot(a, b, preferred_element_type=jnp.float32)

    def run(probs_fn):
        tiles = []
        for u in range(NB_TILES_PER_STEP):
            t = step * NB_TILES_PER_STEP + u
            cls = jnp.where(t == 0, 0, jnp.where(t == n_tiles - 1, 2, 1))
            first_chunk = jnp.clip(t - 1, 0, n_tiles - chunks_per_window)
            tiles.append((u, cls, first_chunk))
        scores = []
        for u, cls, first_chunk in tiles:
            qt = qt_ref[0, :, u * NB_Q:(u + 1) * NB_Q]
            kstart = pl.multiple_of(first_chunk * NB_VCHUNK, NB_VCHUNK)
            s_win = dot(k_ref[pl.ds(kstart, NB_K), :], qt) + bias_ref[0, cls]
            scores.append((s_win, dot(km, qt)))
        probs = [probs_fn(s_win, s_meta) for s_win, s_meta in scores]
        for (u, cls, first_chunk), (p_win, p_meta) in zip(tiles, probs):
            acc = dot(vmt, jnp.concatenate([p_meta, meta_pad], axis=0))
            for j in range(chunks_per_window):
                acc += dot(vt_ref[0, first_chunk + j], p_win[j * NB_VCHUNK:(j + 1) * NB_VCHUNK])
            o = (acc[:HEAD_DIM] * (1.0 / acc[HEAD_DIM:HEAD_DIM + 1])).T
            rows = slice(u * NB_Q, (u + 1) * NB_Q)
            o_ref[rows, :] = (o * g_ref[rows, :]).astype(o_ref.dtype)

    def probs_safe(s_win, s_meta):
        return jnp.exp2(s_win).astype(bf), jnp.exp2(s_meta).astype(bf)

    def probs_max(s_win, s_meta):
        m = jnp.maximum(jnp.max(s_win, axis=0, keepdims=True), jnp.max(s_meta, axis=0, keepdims=True))
        return jnp.exp2(s_win - m).astype(bf), jnp.exp2(s_meta - m).astype(bf)

    safe = safe_ref[0] != 0

    @pl.when(safe)
    def _():
        run(probs_safe)

    @pl.when(jnp.logical_not(safe))
    def _():
        run(probs_max)


def _neighbourhood_attention(safe, qbt, kb, vbt, gate, km, vmt, bias, batch, seq):
    m = gate.shape[1]
    tqb = NB_TILES_PER_STEP * NB_Q
    nq = seq // tqb
    n_vchunks = seq // NB_VCHUNK
    assert vbt.shape[3] == NB_VCHUNK
    qmap = lambda h, b, i: (h, b * nq + i, 0)
    return pl.pallas_call(
        functools.partial(_nbr_kernel, n_tiles=seq // NB_Q),
        grid=(N_HEADS_B, batch, nq),
        in_specs=[
            pl.BlockSpec(memory_space=pltpu.SMEM),
            pl.BlockSpec((1, HEAD_DIM, tqb), lambda h, b, i: (h, 0, b * nq + i)),
            pl.BlockSpec((None, seq, HEAD_DIM), lambda h, b, i: (h, b, 0)),
            pl.BlockSpec((1, n_vchunks, VT_ROWS, NB_VCHUNK), lambda h, b, i: (h, b, 0, 0)),
            pl.BlockSpec((None, META_PAD, HEAD_DIM), lambda h, b, i: (h, 0, 0)),
            pl.BlockSpec((1, 1, VT_ROWS, META_PAD), lambda h, b, i: (h, 0, 0, 0)),
            pl.BlockSpec((1, NB_CLASSES, NB_K, NB_Q), lambda h, b, i: (h, 0, 0, 0)),
            pl.BlockSpec((None, tqb, HEAD_DIM), qmap),
        ],
        out_specs=pl.BlockSpec((None, tqb, HEAD_DIM), qmap),
        out_shape=jax.ShapeDtypeStruct((N_HEADS_B, m, HEAD_DIM), jnp.bfloat16),
        name="neighbourhood_attention",
        compiler_params=pltpu.CompilerParams(
            dimension_semantics=("arbitrary", "arbitrary", "arbitrary"),
            vmem_limit_bytes=VMEM_LIMIT),
    )(safe, qbt, kb, vbt, km, vmt, bias, gate)


def _outproj_kernel(x_ref, ya_ref, yb_ref, w_ref, o_ref):
    yb = jnp.concatenate([yb_ref[h] for h in range(N_HEADS_B)], axis=1)
    y = jnp.dot(ya_ref[...], w_ref[:D_A, :], preferred_element_type=jnp.float32)
    y = y + jnp.dot(yb, w_ref[D_A:, :], preferred_element_type=jnp.float32)
    o_ref[...] = x_ref[...] + y


def _out_projection(x2d, ya, yb, w_out_bf16, tm):
    m = x2d.shape[0]
    row = lambda i: (i, 0)
    return pl.pallas_call(
        _outproj_kernel,
        grid=(m // tm,),
        in_specs=[
            pl.BlockSpec((tm, D_MODEL), row),
            pl.BlockSpec((tm, D_A), row),
            pl.BlockSpec((N_HEADS_B, tm, HEAD_DIM), lambda i: (0, i, 0)),
            pl.BlockSpec((D_A + D_B, D_MODEL), lambda i: (0, 0)),
        ],
        out_specs=pl.BlockSpec((tm, D_MODEL), row),
        out_shape=jax.ShapeDtypeStruct((m, D_MODEL), jnp.float32),
        name="out_projection",
        compiler_params=pltpu.CompilerParams(
            dimension_semantics=("arbitrary",),
            vmem_limit_bytes=VMEM_LIMIT),
    )(x2d, ya, yb, w_out_bf16)


def _rope_tables(rows, cols):
    half = HEAD_DIM // 2
    inv_freq = ROPE_THETA ** (-jnp.arange(0, half, 2, dtype=jnp.float32) / half)
    ang_r = rows.astype(jnp.float32)[:, None] * inv_freq[None, :]
    ang_c = cols.astype(jnp.float32)[:, None] * inv_freq[None, :]
    n_r, n_c = rows.shape[0], cols.shape[0]

    def grid(r_part, c_part):
        r_part = jnp.broadcast_to(r_part[:, None, :], (n_r, n_c, r_part.shape[-1]))
        c_part = jnp.broadcast_to(c_part[None, :, :], (n_r, n_c, c_part.shape[-1]))
        return jnp.concatenate([r_part, c_part], axis=-1).reshape(n_r * n_c, HEAD_DIM)

    cr, sr, cc, sc = jnp.cos(ang_r), jnp.sin(ang_r), jnp.cos(ang_c), jnp.sin(ang_c)
    zr, zc = jnp.zeros_like(sr), jnp.zeros_like(sc)
    cos = grid(jnp.concatenate([cr, cr], -1), jnp.concatenate([cc, cc], -1))
    sin_a = grid(jnp.concatenate([-sr, zr], -1), jnp.concatenate([-sc, zc], -1))
    sin_b = grid(jnp.concatenate([zr, sr], -1), jnp.concatenate([zc, sc], -1))
    return cos, sin_a, sin_b


def _nbr_bias_tables(rpb):
    rows = 32
    tile_r = np.array([0, 2 * NB_QROWS, rows - NB_QROWS])[:, None, None]
    dr = np.arange(NB_QROWS)[None, :, None]
    jj = np.arange(NB_KROWS)[None, None, :]
    k0 = np.clip(tile_r - NB_QROWS, 0, rows - NB_KROWS)
    r = tile_r + dr
    kr = k0 + jj
    r0 = np.clip(r - WIN_R // 2, 0, rows - WIN_R)
    row_valid = (kr >= r0) & (kr < r0 + WIN_R)
    off_r = np.clip(kr - r + (WIN_R - 1), 0, 2 * WIN_R - 2)
    n_off_c = 2 * WIN_C - 1
    pos = np.arange(GRID_W)
    onehot = (pos[None, :, None] - pos[None, None, :] + (WIN_C - 1)
              == np.arange(n_off_c)[:, None, None]).astype(np.float32)
    toeplitz = jnp.einsum('hro,okc->hrkc', rpb.astype(jnp.float32) * LOG2E, onehot,
                          precision=lax.Precision.HIGHEST)

    def assemble(toep_ref, out_ref):
        kc = lax.broadcasted_iota(jnp.int32, (GRID_W, GRID_W), 0)
        c = lax.broadcasted_iota(jnp.int32, (GRID_W, GRID_W), 1)
        c0 = jnp.clip(c - WIN_C // 2, 0, GRID_W - WIN_C)
        col_valid = (kc >= c0) & (kc < c0 + WIN_C)
        masked = jnp.full((GRID_W, GRID_W), MASK_VALUE, jnp.float32)
        for cls in range(NB_CLASSES):
            for j in range(NB_KROWS):
                blocks = [jnp.where(col_valid, toep_ref[0, int(off_r[cls, d, j])], MASK_VALUE)
                          if row_valid[cls, d, j] else masked for d in range(NB_QROWS)]
                out_ref[0, cls, j * GRID_W:(j + 1) * GRID_W, :] = jnp.concatenate(blocks, axis=1)

    return pl.pallas_call(
        assemble,
        grid=(N_HEADS_B,),
        in_specs=[pl.BlockSpec((1, 2 * WIN_R - 1, GRID_W, GRID_W), lambda h: (h, 0, 0, 0))],
        out_specs=pl.BlockSpec((1, NB_CLASSES, NB_K, NB_Q), lambda h: (h, 0, 0, 0)),
        out_shape=jax.ShapeDtypeStruct((N_HEADS_B, NB_CLASSES, NB_K, NB_Q), jnp.float32),
        name="nbr_bias_table",
    )(toeplitz)


def kernel(x_prompt, x_sample, meta_tokens, norm_w, w_in, q_norm_a, k_norm_a, q_norm_b, k_norm_b, rpb, w_out):
    w_in_bf = w_in[0].astype(jnp.bfloat16)
    w_out_bf = w_out[0].astype(jnp.bfloat16)
    nw = norm_w[0].reshape(1, D_MODEL)
    gains = [g[0].reshape(1, HEAD_DIM) for g in (q_norm_a, k_norm_a, q_norm_b, k_norm_b)]
    bias = _nbr_bias_tables(rpb[0])

    meta_tabs = _rope_tables(jnp.full((1,), -1, jnp.int32), jnp.arange(META_PAD, dtype=jnp.int32))
    meta_x = jnp.pad(meta_tokens, ((0, META_PAD - N_META), (0, 0)))
    score_bound_a = (HEAD_DIM * Q_SCALE * jnp.max(jnp.abs(q_norm_a[0])) * jnp.max(jnp.abs(k_norm_a[0])))
    safe_a = (score_bound_a <= SAFE_SCORE_BOUND).astype(jnp.int32).reshape(1)
    score_bound_b = (HEAD_DIM * Q_SCALE * jnp.max(jnp.abs(q_norm_b[0])) * jnp.max(jnp.abs(k_norm_b[0]))
                     + LOG2E * jnp.max(jnp.abs(rpb[0])))
    safe_b = (score_bound_b <= SAFE_SCORE_BOUND).astype(jnp.int32).reshape(1)
    meta_out = _in_projection(meta_x, META_PAD, META_PAD, META_PAD, nw, w_in_bf, gains, meta_tabs)
    _, km_a, vmt_a, _, _, km_b, vmt_b, _ = meta_out

    max_seq = max(x_prompt.shape[1], x_sample.shape[1])
    tabs = _rope_tables(jnp.arange(max_seq // GRID_W, dtype=jnp.int32),
                        jnp.arange(GRID_W, dtype=jnp.int32))

    def encode(x):
        batch, seq, _ = x.shape
        x2d = x.reshape(batch * seq, D_MODEL)
        qat, ka, vat, ga, qbt, kb, vbt, gb = _in_projection(
            x2d, seq, IN_ROW_TILE, GQA_KEY_CHUNK, nw, w_in_bf, gains, tabs)
        ya = _global_attention(safe_a, qat, ka, vat, ga, km_a, vmt_a, batch, seq, GQA_Q_TILE)
        yb = _neighbourhood_attention(safe_b, qbt, kb, vbt, gb, km_b, vmt_b, bias, batch, seq)
        y = _out_projection(x2d, ya, yb, w_out_bf, OUT_ROW_TILE)
        return y.reshape(batch, seq, D_MODEL)

    return (encode(x_prompt), encode(x_sample))
```

```python
import functools
import math

import jax
import jax.numpy as jnp
import numpy as np
from jax import lax
from jax.experimental import pallas as pl
from jax.experimental.pallas import tpu as pltpu

D_MODEL = 2048
HEAD_DIM = 128
N_HEADS_A = 8
N_KV_A = 2
GROUP_A = N_HEADS_A // N_KV_A
N_HEADS_B = 8
D_A = N_HEADS_A * HEAD_DIM
D_KV_A = N_KV_A * HEAD_DIM
D_B = N_HEADS_B * HEAD_DIM
D_IN = D_A + 2 * D_KV_A + D_A + 4 * D_B
N_META = 16
GRID_W = 64
WIN_R = 8
WIN_C = 16
ROPE_THETA = 10000.0
EPS = 1e-6

LOG2E = math.log2(math.e)
Q_SCALE = HEAD_DIM ** -0.5 * LOG2E
MASK_VALUE = -1e30

LANES = 128
BF16_SUBLANE_TILE = 16
VT_EXTRA_ROWS = BF16_SUBLANE_TILE
VT_ROWS = HEAD_DIM + VT_EXTRA_ROWS
SAFE_SCORE_BOUND = 60.0
META_PAD = LANES
VMEM_LIMIT = 56 * 1024 * 1024

TN = 512
N_COL_TILES = D_IN // TN
NB_QROWS = 4
NB_Q = NB_QROWS * GRID_W
NB_KROWS = 3 * NB_QROWS
NB_K = NB_KROWS * GRID_W
NB_CLASSES = 3

IN_ROW_TILE = 256
OUT_ROW_TILE = 512
GQA_KEY_CHUNK = 1024
GQA_Q_TILE = 512
NB_TILES_PER_STEP = 8
NB_VCHUNK = IN_ROW_TILE


def _silu(z):
    return z * (1.0 / (1.0 + jnp.exp(-z)))


def _head_norm(a, gain):
    ms = jnp.mean(a * a, axis=-1, keepdims=True)
    return a * lax.rsqrt(ms + EPS) * gain


def _rope(y, cos, sin_a, sin_b):
    return y * cos + pltpu.roll(y, 96, 1) * sin_a + pltpu.roll(y, 32, 1) * sin_b


def _inproj_kernel(*refs, tile_starts):
    n_src = len(tile_starts) - 1
    x_refs = refs[:n_src]
    (nw_ref, w_ref, gqa_ref, gka_ref, gqb_ref, gkb_ref, cos_ref, sa_ref, sb_ref,
     qat_ref, ka_ref, vat_ref, za_ref, qbt_ref, kb_ref, vbt_ref, zb_ref, xn_ref) = refs[n_src:]
    step = pl.program_id(0)
    for g, x_ref in enumerate(x_refs):
        @pl.when((step >= tile_starts[g]) & (step < tile_starts[g + 1]))
        def _(x_ref=x_ref):
            x = x_ref[...]
            ms = jnp.mean(x * x, axis=-1, keepdims=True)
            xn_ref[...] = (x * lax.rsqrt(ms + EPS) * nw_ref[...]).astype(jnp.bfloat16)

    def column_tile(j):
        return jnp.dot(xn_ref[...], w_ref[:, j * TN:(j + 1) * TN],
                       preferred_element_type=jnp.float32)

    def normed_roped(a, gain_ref):
        return _rope(_head_norm(a, gain_ref[...]), cos_ref[...], sa_ref[...], sb_ref[...])

    def store_heads(ref, first, vals):
        for h, v in enumerate(vals):
            ref[first + h] = v.astype(ref.dtype)

    heads_per_tile = TN // HEAD_DIM

    def store_values_t(ref, first, vals):
        ones_row = (lax.broadcasted_iota(jnp.int32, (VT_EXTRA_ROWS, vals[0].shape[0]), 0) == 0)
        for h, a in enumerate(vals):
            ref[first + h, 0, :HEAD_DIM, :] = a.T.astype(ref.dtype)
            ref[first + h, 0, HEAD_DIM:, :] = ones_row.astype(ref.dtype)

    def epilogue(j, acc):
        heads = [acc[:, h * HEAD_DIM:(h + 1) * HEAD_DIM] for h in range(heads_per_tile)]
        if j < 2:
            for h, a in enumerate(heads):
                q = normed_roped(a, gqa_ref) * Q_SCALE
                qat_ref[j * heads_per_tile + h] = q.T.astype(qat_ref.dtype)
        elif j == 2:
            store_heads(ka_ref, 0, [normed_roped(a, gka_ref) for a in heads[:N_KV_A]])
            store_values_t(vat_ref, 0, heads[N_KV_A:])
        elif j < 5:
            za_ref[:, (j - 3) * TN:(j - 2) * TN] = _silu(acc)
        elif j < 7:
            for h, a in enumerate(heads):
                q = _head_norm(a, gqb_ref[...]) * Q_SCALE
                qbt_ref[(j - 5) * heads_per_tile + h] = q.T.astype(qbt_ref.dtype)
        elif j < 9:
            store_heads(kb_ref, (j - 7) * heads_per_tile,
                        [_head_norm(a, gkb_ref[...]) for a in heads])
        elif j < 11:
            store_values_t(vbt_ref, (j - 9) * heads_per_tile, heads)
        else:
            store_heads(zb_ref, (j - 11) * heads_per_tile, [_silu(a) for a in heads])

    acc = column_tile(0)
    for j in range(N_COL_TILES):
        nxt = column_tile(j + 1) if j + 1 < N_COL_TILES else None
        epilogue(j, acc)
        acc = nxt


def _in_projection(sources, tm, chunk, norm_w, w_in_bf16, gains, tabs):
    tiles = [x.shape[0] // tm for x, _, _ in sources]
    starts = [sum(tiles[:g]) for g in range(len(tiles) + 1)]
    m = starts[-1] * tm
    sub = chunk // tm
    cos, sin_a, sin_b = tabs
    row = lambda i: (i, 0)
    const = lambda i: (0, 0)
    bf, f32 = jnp.bfloat16, jnp.float32
    head_rows = lambda i: (0, i, 0)

    def x_spec(g):
        return pl.BlockSpec((tm, D_MODEL), lambda i: (jnp.clip(i - starts[g], 0, tiles[g] - 1), 0))

    def tab_tile(i):
        idx = 0
        for g, (_, first, period) in enumerate(sources):
            inside = (i >= starts[g]) & (i < starts[g + 1])
            idx = jnp.where(inside, first + (i - starts[g]) % period, idx)
        return idx, 0

    out_shape = [
        jax.ShapeDtypeStruct((N_HEADS_A, HEAD_DIM, m), bf),
        jax.ShapeDtypeStruct((N_KV_A, m, HEAD_DIM), bf),
        jax.ShapeDtypeStruct((N_KV_A, -(-m // chunk), VT_ROWS, chunk), bf),
        jax.ShapeDtypeStruct((m, D_A), f32),
        jax.ShapeDtypeStruct((N_HEADS_B, HEAD_DIM, m), bf),
        jax.ShapeDtypeStruct((N_HEADS_B, m, HEAD_DIM), bf),
        jax.ShapeDtypeStruct((N_HEADS_B, m // tm, VT_ROWS, tm), bf),
        jax.ShapeDtypeStruct((N_HEADS_B, m, HEAD_DIM), f32),
    ]
    out_specs = [
        pl.BlockSpec((N_HEADS_A, HEAD_DIM, tm), lambda i: (0, 0, i)),
        pl.BlockSpec((N_KV_A, tm, HEAD_DIM), head_rows),
        pl.BlockSpec((N_KV_A, 1, VT_ROWS, tm), lambda i: (0, i // sub, 0, i % sub)),
        pl.BlockSpec((tm, D_A), row),
        pl.BlockSpec((N_HEADS_B, HEAD_DIM, tm), lambda i: (0, 0, i)),
        pl.BlockSpec((N_HEADS_B, tm, HEAD_DIM), head_rows),
        pl.BlockSpec((N_HEADS_B, 1, VT_ROWS, tm), lambda i: (0, i, 0, 0)),
        pl.BlockSpec((N_HEADS_B, tm, HEAD_DIM), head_rows),
    ]
    gain_spec = pl.BlockSpec((1, HEAD_DIM), const)
    tab_spec = pl.BlockSpec((tm, HEAD_DIM), tab_tile)
    return pl.pallas_call(
        functools.partial(_inproj_kernel, tile_starts=tuple(starts)),
        grid=(m // tm,),
        in_specs=[x_spec(g) for g in range(len(sources))] + [
            pl.BlockSpec((1, D_MODEL), const),
            pl.BlockSpec((D_MODEL, D_IN), const, pipeline_mode=pl.Buffered(1)),
            gain_spec, gain_spec, gain_spec, gain_spec,
            tab_spec, tab_spec, tab_spec,
        ],
        out_specs=out_specs,
        out_shape=out_shape,
        scratch_shapes=[pltpu.VMEM((tm, D_MODEL), bf)],
        name="in_projection",
        compiler_params=pltpu.CompilerParams(
            dimension_semantics=("arbitrary",),
            vmem_limit_bytes=VMEM_LIMIT),
    )(*[x for x, _, _ in sources], norm_w, w_in_bf16, *gains, cos, sin_a, sin_b)


def _nt_dot(a, b):
    return lax.dot_general(a, b, (((1,), (1,)), ((), ())), preferred_element_type=jnp.float32)


GQA_SLAB_WIDTH = 512


def _gqa_kernel(safe_ref, qt_ref, k_ref, vt_ref, km_ref, vmt_ref, g_ref, o_ref, acc_ref, s_ref):
    tq = qt_ref.shape[2]
    n_chunks, chunk = vt_ref.shape[1], vt_ref.shape[3]
    width = acc_ref.shape[2]
    n_slabs = acc_ref.shape[0]
    heads_per_slab = width // tq
    km = km_ref[:N_META, :]
    vmt = vmt_ref[0, 0]
    meta_pad = jnp.zeros((META_PAD - N_META, width), jnp.bfloat16)

    def meta_values(p):
        return dot(vmt, jnp.concatenate([p.astype(jnp.bfloat16), meta_pad], axis=0))
    qts = [jnp.concatenate([qt_ref[i * heads_per_slab + j] for j in range(heads_per_slab)], axis=1)
           if heads_per_slab > 1 else qt_ref[i] for i in range(n_slabs)]

    def dot(a, b):
        return jnp.dot(a, b, preferred_element_type=jnp.float32)

    def keys(c):
        if isinstance(c, int):
            return k_ref[c * chunk:(c + 1) * chunk, :]
        return k_ref[pl.ds(pl.multiple_of(c * chunk, chunk), chunk), :]

    def attend(init, consume):
        def chunk_stages(c, state, has_next):
            state = list(state)
            k = keys(c)
            vt = vt_ref[0, c]
            for i in range(n_slabs):
                if i + 1 < n_slabs:
                    s_ref[(i + 1) % 2] = dot(k, qts[i + 1])
                elif has_next:
                    s_ref[0] = dot(keys(c + 1), qts[0])
                state[i] = consume(i, s_ref[i % 2], vt, state[i])
            return tuple(state)

        s_ref[0] = dot(keys(0), qts[0])
        s_meta = [dot(km, qt) for qt in qts]
        state = tuple(init(i, s) for i, s in enumerate(s_meta))
        state = chunk_stages(0, state, n_chunks > 1)
        if n_chunks > 1:
            state = lax.fori_loop(1, n_chunks - 1, lambda c, st: chunk_stages(c, st, True), state)
            chunk_stages(n_chunks - 1, state, False)
        for i in range(n_slabs):
            acc = acc_ref[i]
            o_slab = acc[:HEAD_DIM] * (1.0 / acc[HEAD_DIM:HEAD_DIM + 1])
            for j in range(heads_per_slab):
                h = i * heads_per_slab + j
                sl = slice(h * HEAD_DIM, (h + 1) * HEAD_DIM)
                o = o_slab[:, j * tq:(j + 1) * tq].T
                o_ref[:, sl] = (o * g_ref[:, sl]).astype(o_ref.dtype)

    safe = safe_ref[0] != 0

    @pl.when(safe)
    def _():
        def init(i, s):
            acc_ref[i] = meta_values(jnp.exp2(s))
            return 0

        def consume(i, s, vt, state):
            acc_ref[i] += dot(vt, jnp.exp2(s).astype(jnp.bfloat16))
            return state

        attend(init, consume)

    @pl.when(jnp.logical_not(safe))
    def _():
        def init(i, s):
            m = jnp.max(s, axis=0, keepdims=True)
            acc_ref[i] = meta_values(jnp.exp2(s - m))
            return m

        def consume(i, s, vt, m):
            m_new = jnp.maximum(m, jnp.max(s, axis=0, keepdims=True))
            p = jnp.exp2(s - m_new)
            acc_ref[i] = jnp.exp2(m - m_new) * acc_ref[i] + dot(vt, p.astype(jnp.bfloat16))
            return m_new

        attend(init, consume)


def _global_attention(safe, qat, ka, vat, gate, batch, seq, tq, row0, meta_row0):
    nq = seq // tq
    chunk = vat.shape[3]
    n_chunks = seq // chunk
    q0, s0 = row0 // tq, row0 // seq
    qmap = lambda b, g, i: (b * nq + i, g)
    width = GROUP_A * HEAD_DIM
    return pl.pallas_call(
        _gqa_kernel,
        grid=(batch, N_KV_A, nq),
        in_specs=[
            pl.BlockSpec(memory_space=pltpu.SMEM),
            pl.BlockSpec((GROUP_A, HEAD_DIM, tq), lambda b, g, i: (g, 0, q0 + b * nq + i)),
            pl.BlockSpec((None, seq, HEAD_DIM), lambda b, g, i: (g, s0 + b, 0)),
            pl.BlockSpec((1, n_chunks, VT_ROWS, chunk), lambda b, g, i: (g, s0 + b, 0, 0)),
            pl.BlockSpec((None, META_PAD, HEAD_DIM), lambda b, g, i: (g, meta_row0 // META_PAD, 0)),
            pl.BlockSpec((1, 1, VT_ROWS, META_PAD),
                         lambda b, g, i: (g, meta_row0 // chunk, 0, meta_row0 % chunk // META_PAD)),
            pl.BlockSpec((tq, width), lambda b, g, i: (q0 + b * nq + i, g)),
        ],
        out_specs=pl.BlockSpec((tq, width), qmap),
        out_shape=jax.ShapeDtypeStruct((batch * seq, D_A), jnp.bfloat16),
        scratch_shapes=[pltpu.VMEM((GROUP_A * tq // GQA_SLAB_WIDTH, VT_ROWS, GQA_SLAB_WIDTH), jnp.float32),
                        pltpu.VMEM((2, chunk, GQA_SLAB_WIDTH), jnp.float32)],
        name="global_attention",
        compiler_params=pltpu.CompilerParams(
            dimension_semantics=("arbitrary", "arbitrary", "arbitrary"),
            vmem_limit_bytes=VMEM_LIMIT),
    )(safe, qat, ka, vat, ka, vat, gate)


def _nbr_kernel(safe_ref, qt_ref, k_ref, vt_ref, km_ref, vmt_ref, bias_ref, g_ref, o_ref, *, n_tiles):
    step = pl.program_id(2)
    bf = jnp.bfloat16
    km = km_ref[:N_META, :]
    vmt = vmt_ref[0, 0]
    meta_pad = jnp.zeros((META_PAD - N_META, NB_Q), bf)
    chunks_per_window = NB_K // NB_VCHUNK

    def dot(a, b):
        return jnp.dot(a, b, preferred_element_type=jnp.float32)

    def run(probs_fn):
        tiles = []
        for u in range(NB_TILES_PER_STEP):
            t = step * NB_TILES_PER_STEP + u
            cls = jnp.where(t == 0, 0, jnp.where(t == n_tiles - 1, 2, 1))
            first_chunk = jnp.clip(t - 1, 0, n_tiles - chunks_per_window)
            tiles.append((u, cls, first_chunk))
        scores = []
        for u, cls, first_chunk in tiles:
            qt = qt_ref[0, :, u * NB_Q:(u + 1) * NB_Q]
            kstart = pl.multiple_of(first_chunk * NB_VCHUNK, NB_VCHUNK)
            s_win = dot(k_ref[pl.ds(kstart, NB_K), :], qt) + bias_ref[0, cls]
            scores.append((s_win, dot(km, qt)))
        probs = [probs_fn(s_win, s_meta) for s_win, s_meta in scores]
        for (u, cls, first_chunk), (p_win, p_meta) in zip(tiles, probs):
            acc = dot(vmt, jnp.concatenate([p_meta, meta_pad], axis=0))
            for j in range(chunks_per_window):
                acc += dot(vt_ref[0, first_chunk + j], p_win[j * NB_VCHUNK:(j + 1) * NB_VCHUNK])
            o = (acc[:HEAD_DIM] * (1.0 / acc[HEAD_DIM:HEAD_DIM + 1])).T
            rows = slice(u * NB_Q, (u + 1) * NB_Q)
            o_ref[rows, :] = (o * g_ref[rows, :]).astype(o_ref.dtype)

    def probs_safe(s_win, s_meta):
        return jnp.exp2(s_win).astype(bf), jnp.exp2(s_meta).astype(bf)

    def probs_max(s_win, s_meta):
        m = jnp.maximum(jnp.max(s_win, axis=0, keepdims=True), jnp.max(s_meta, axis=0, keepdims=True))
        return jnp.exp2(s_win - m).astype(bf), jnp.exp2(s_meta - m).astype(bf)

    safe = safe_ref[0] != 0

    @pl.when(safe)
    def _():
        run(probs_safe)

    @pl.when(jnp.logical_not(safe))
    def _():
        run(probs_max)


def _neighbourhood_attention(safe, qbt, kb, vbt, gate, bias, batch, seq, row0, meta_row0):
    tqb = NB_TILES_PER_STEP * NB_Q
    nq = seq // tqb
    n_vchunks = seq // NB_VCHUNK
    assert vbt.shape[3] == NB_VCHUNK
    q0, s0 = row0 // tqb, row0 // seq
    qmap = lambda h, b, i: (h, b * nq + i, 0)
    return pl.pallas_call(
        functools.partial(_nbr_kernel, n_tiles=seq // NB_Q),
        grid=(N_HEADS_B, batch, nq),
        in_specs=[
            pl.BlockSpec(memory_space=pltpu.SMEM),
            pl.BlockSpec((1, HEAD_DIM, tqb), lambda h, b, i: (h, 0, q0 + b * nq + i)),
            pl.BlockSpec((None, seq, HEAD_DIM), lambda h, b, i: (h, s0 + b, 0)),
            pl.BlockSpec((1, n_vchunks, VT_ROWS, NB_VCHUNK), lambda h, b, i: (h, s0 + b, 0, 0)),
            pl.BlockSpec((None, META_PAD, HEAD_DIM), lambda h, b, i: (h, meta_row0 // META_PAD, 0)),
            pl.BlockSpec((1, 1, VT_ROWS, META_PAD), lambda h, b, i: (h, meta_row0 // NB_VCHUNK, 0, 0)),
            pl.BlockSpec((1, NB_CLASSES, NB_K, NB_Q), lambda h, b, i: (h, 0, 0, 0)),
            pl.BlockSpec((None, tqb, HEAD_DIM), lambda h, b, i: (h, q0 + b * nq + i, 0)),
        ],
        out_specs=pl.BlockSpec((None, tqb, HEAD_DIM), qmap),
        out_shape=jax.ShapeDtypeStruct((N_HEADS_B, batch * seq, HEAD_DIM), jnp.bfloat16),
        name="neighbourhood_attention",
        compiler_params=pltpu.CompilerParams(
            dimension_semantics=("arbitrary", "arbitrary", "arbitrary"),
            vmem_limit_bytes=VMEM_LIMIT),
    )(safe, qbt, kb, vbt, kb, vbt, bias, gate)


def _outproj_kernel(x_ref, ya_ref, yb_ref, w_ref, o_ref):
    yb = jnp.concatenate([yb_ref[h] for h in range(N_HEADS_B)], axis=1)
    y = jnp.dot(ya_ref[...], w_ref[:D_A, :], preferred_element_type=jnp.float32)
    y = y + jnp.dot(yb, w_ref[D_A:, :], preferred_element_type=jnp.float32)
    o_ref[...] = x_ref[...] + y


def _out_projection(x2d, ya, yb, w_out_bf16, tm):
    m = x2d.shape[0]
    row = lambda i: (i, 0)
    return pl.pallas_call(
        _outproj_kernel,
        grid=(m // tm,),
        in_specs=[
            pl.BlockSpec((tm, D_MODEL), row),
            pl.BlockSpec((tm, D_A), row),
            pl.BlockSpec((N_HEADS_B, tm, HEAD_DIM), lambda i: (0, i, 0)),
            pl.BlockSpec((D_A + D_B, D_MODEL), lambda i: (0, 0)),
        ],
        out_specs=pl.BlockSpec((tm, D_MODEL), row),
        out_shape=jax.ShapeDtypeStruct((m, D_MODEL), jnp.float32),
        name="out_projection",
        compiler_params=pltpu.CompilerParams(
            dimension_semantics=("arbitrary",),
            vmem_limit_bytes=VMEM_LIMIT),
    )(x2d, ya, yb, w_out_bf16)


def _rope_tables(rows, cols):
    half = HEAD_DIM // 2
    inv_freq = ROPE_THETA ** (-jnp.arange(0, half, 2, dtype=jnp.float32) / half)
    ang_r = rows.astype(jnp.float32)[:, None] * inv_freq[None, :]
    ang_c = cols.astype(jnp.float32)[:, None] * inv_freq[None, :]
    n_r, n_c = rows.shape[0], cols.shape[0]

    def grid(r_part, c_part):
        r_part = jnp.broadcast_to(r_part[:, None, :], (n_r, n_c, r_part.shape[-1]))
        c_part = jnp.broadcast_to(c_part[None, :, :], (n_r, n_c, c_part.shape[-1]))
        return jnp.concatenate([r_part, c_part], axis=-1).reshape(n_r * n_c, HEAD_DIM)

    cr, sr, cc, sc = jnp.cos(ang_r), jnp.sin(ang_r), jnp.cos(ang_c), jnp.sin(ang_c)
    zr, zc = jnp.zeros_like(sr), jnp.zeros_like(sc)
    cos = grid(jnp.concatenate([cr, cr], -1), jnp.concatenate([cc, cc], -1))
    sin_a = grid(jnp.concatenate([-sr, zr], -1), jnp.concatenate([-sc, zc], -1))
    sin_b = grid(jnp.concatenate([zr, sr], -1), jnp.concatenate([zc, sc], -1))
    return cos, sin_a, sin_b


def _nbr_bias_tables(rpb):
    rows = 32
    tile_r = np.array([0, 2 * NB_QROWS, rows - NB_QROWS])[:, None, None]
    dr = np.arange(NB_QROWS)[None, :, None]
    jj = np.arange(NB_KROWS)[None, None, :]
    k0 = np.clip(tile_r - NB_QROWS, 0, rows - NB_KROWS)
    r = tile_r + dr
    kr = k0 + jj
    r0 = np.clip(r - WIN_R // 2, 0, rows - WIN_R)
    row_valid = (kr >= r0) & (kr < r0 + WIN_R)
    off_r = np.clip(kr - r + (WIN_R - 1), 0, 2 * WIN_R - 2)
    n_off_c = 2 * WIN_C - 1
    pos = np.arange(GRID_W)
    onehot = (pos[None, :, None] - pos[None, None, :] + (WIN_C - 1)
              == np.arange(n_off_c)[:, None, None]).astype(np.float32)
    toeplitz = jnp.einsum('hro,okc->hrkc', rpb.astype(jnp.float32) * LOG2E, onehot,
                          precision=lax.Precision.HIGHEST)

    def assemble(toep_ref, out_ref):
        kc = lax.broadcasted_iota(jnp.int32, (GRID_W, GRID_W), 0)
        c = lax.broadcasted_iota(jnp.int32, (GRID_W, GRID_W), 1)
        c0 = jnp.clip(c - WIN_C // 2, 0, GRID_W - WIN_C)
        col_valid = (kc >= c0) & (kc < c0 + WIN_C)
        masked = jnp.full((GRID_W, GRID_W), MASK_VALUE, jnp.float32)
        for cls in range(NB_CLASSES):
            for j in range(NB_KROWS):
                blocks = [jnp.where(col_valid, toep_ref[0, int(off_r[cls, d, j])], MASK_VALUE)
                          if row_valid[cls, d, j] else masked for d in range(NB_QROWS)]
                out_ref[0, cls, j * GRID_W:(j + 1) * GRID_W, :] = jnp.concatenate(blocks, axis=1)

    return pl.pallas_call(
        assemble,
        grid=(N_HEADS_B,),
        in_specs=[pl.BlockSpec((1, 2 * WIN_R - 1, GRID_W, GRID_W), lambda h: (h, 0, 0, 0))],
        out_specs=pl.BlockSpec((1, NB_CLASSES, NB_K, NB_Q), lambda h: (h, 0, 0, 0)),
        out_shape=jax.ShapeDtypeStruct((N_HEADS_B, NB_CLASSES, NB_K, NB_Q), jnp.float32),
        name="nbr_bias_table",
    )(toeplitz)


def kernel(x_prompt, x_sample, meta_tokens, norm_w, w_in, q_norm_a, k_norm_a, q_norm_b, k_norm_b, rpb, w_out):
    w_in_bf = w_in[0].astype(jnp.bfloat16)
    w_out_bf = w_out[0].astype(jnp.bfloat16)
    nw = norm_w[0].reshape(1, D_MODEL)
    gains = [g[0].reshape(1, HEAD_DIM) for g in (q_norm_a, k_norm_a, q_norm_b, k_norm_b)]
    bias = _nbr_bias_tables(rpb[0])

    score_bound_a = (HEAD_DIM * Q_SCALE * jnp.max(jnp.abs(q_norm_a[0])) * jnp.max(jnp.abs(k_norm_a[0])))
    safe_a = (score_bound_a <= SAFE_SCORE_BOUND).astype(jnp.int32).reshape(1)
    score_bound_b = (HEAD_DIM * Q_SCALE * jnp.max(jnp.abs(q_norm_b[0])) * jnp.max(jnp.abs(k_norm_b[0]))
                     + LOG2E * jnp.max(jnp.abs(rpb[0])))
    safe_b = (score_bound_b <= SAFE_SCORE_BOUND).astype(jnp.int32).reshape(1)
    groups = (x_prompt, x_sample)
    max_seq = max(x.shape[1] for x in groups)
    col = jnp.arange(GRID_W, dtype=jnp.int32)
    seq_tabs = _rope_tables(jnp.arange(max_seq // GRID_W, dtype=jnp.int32), col)
    meta_tabs = _rope_tables(jnp.full((1,), -1, jnp.int32), jnp.arange(IN_ROW_TILE, dtype=jnp.int32))
    tabs = [jnp.concatenate([s, t], axis=0) for s, t in zip(seq_tabs, meta_tabs)]
    meta_x = jnp.pad(meta_tokens, ((0, IN_ROW_TILE - N_META), (0, 0)))

    xs = [x.reshape(x.shape[0] * x.shape[1], D_MODEL) for x in groups]
    sources = [(x2d, 0, x.shape[1] // IN_ROW_TILE) for x2d, x in zip(xs, groups)]
    sources.append((meta_x, max_seq // IN_ROW_TILE, 1))
    qat, ka, vat, ga, qbt, kb, vbt, gb = _in_projection(
        sources, IN_ROW_TILE, GQA_KEY_CHUNK, nw, w_in_bf, gains, tabs)
    meta_row0 = sum(x2d.shape[0] for x2d in xs)

    outs, row0 = [], 0
    for x, x2d in zip(groups, xs):
        batch, seq, _ = x.shape
        ya = _global_attention(safe_a, qat, ka, vat, ga, batch, seq, GQA_Q_TILE, row0, meta_row0)
        yb = _neighbourhood_attention(safe_b, qbt, kb, vbt, gb, bias, batch, seq, row0, meta_row0)
        y = _out_projection(x2d, ya, yb, w_out_bf, OUT_ROW_TILE)
        outs.append(y.reshape(batch, seq, D_MODEL))
        row0 += x2d.shape[0]
    return tuple(outs)
```

```python
import functools
import math

import jax
import jax.numpy as jnp
import numpy as np
from jax import lax
from jax.experimental import pallas as pl
from jax.experimental.pallas import tpu as pltpu

D_MODEL = 2048
HEAD_DIM = 128
N_HEADS_A = 8
N_KV_A = 2
GROUP_A = N_HEADS_A // N_KV_A
N_HEADS_B = 8
D_A = N_HEADS_A * HEAD_DIM
D_KV_A = N_KV_A * HEAD_DIM
D_B = N_HEADS_B * HEAD_DIM
D_IN = D_A + 2 * D_KV_A + D_A + 4 * D_B
N_META = 16
GRID_W = 64
WIN_R = 8
WIN_C = 16
ROPE_THETA = 10000.0
EPS = 1e-6

LOG2E = math.log2(math.e)
Q_SCALE = HEAD_DIM ** -0.5 * LOG2E
MASK_VALUE = -1e30

LANES = 128
BF16_SUBLANE_TILE = 16
VT_EXTRA_ROWS = BF16_SUBLANE_TILE
VT_ROWS = HEAD_DIM + VT_EXTRA_ROWS
SAFE_SCORE_BOUND = 60.0
META_PAD = LANES
VMEM_LIMIT = 56 * 1024 * 1024

TN = 512
N_COL_TILES = D_IN // TN
NB_QROWS = 4
NB_Q = NB_QROWS * GRID_W
NB_KROWS = 3 * NB_QROWS
NB_K = NB_KROWS * GRID_W
NB_CLASSES = 3

IN_ROW_TILE = 256
OUT_ROW_TILE = 512
GQA_KEY_CHUNK = 1024
GQA_Q_TILE = 512
GQA_SLAB_WIDTH = 512
NB_TILES_PER_STEP = 8
NB_VCHUNK = IN_ROW_TILE


def _silu(z):
    return z * (1.0 / (1.0 + jnp.exp(-z)))


def _head_norm(a, gain):
    ms = jnp.mean(a * a, axis=-1, keepdims=True)
    return a * lax.rsqrt(ms + EPS) * gain


def _rope(y, cos, sin_a, sin_b):
    return y * cos + pltpu.roll(y, 96, 1) * sin_a + pltpu.roll(y, 32, 1) * sin_b


def _dot(a, b):
    return jnp.dot(a, b, preferred_element_type=jnp.float32)


def _inproj_kernel(x_ref, nw_ref, w_ref, gqa_ref, gka_ref, gqb_ref, gkb_ref,
                   cos_ref, sa_ref, sb_ref,
                   qat_ref, ka_ref, vat_ref, za_ref, qbt_ref, kb_ref, vbt_ref, zb_ref,
                   xn_ref):
    x = x_ref[...]
    ms = jnp.mean(x * x, axis=-1, keepdims=True)
    xn_ref[...] = (x * lax.rsqrt(ms + EPS) * nw_ref[...]).astype(jnp.bfloat16)

    def column_tile(j):
        return _dot(xn_ref[...], w_ref[:, j * TN:(j + 1) * TN])

    def normed_roped(a, gain_ref):
        return _rope(_head_norm(a, gain_ref[...]), cos_ref[...], sa_ref[...], sb_ref[...])

    def store_heads(ref, first, vals):
        for h, v in enumerate(vals):
            ref[:, (first + h) * HEAD_DIM:(first + h + 1) * HEAD_DIM] = v.astype(ref.dtype)

    heads_per_tile = TN // HEAD_DIM

    def store_values_t(ref, first, vals):
        ones_row = (lax.broadcasted_iota(jnp.int32, (VT_EXTRA_ROWS, vals[0].shape[0]), 0) == 0)
        for h, a in enumerate(vals):
            ref[first + h, 0, :HEAD_DIM, :] = a.T.astype(ref.dtype)
            ref[first + h, 0, HEAD_DIM:, :] = ones_row.astype(ref.dtype)

    def epilogue(j, acc):
        heads = [acc[:, h * HEAD_DIM:(h + 1) * HEAD_DIM] for h in range(heads_per_tile)]
        if j < 2:
            for h, a in enumerate(heads):
                q = normed_roped(a, gqa_ref) * Q_SCALE
                qat_ref[j * heads_per_tile + h] = q.T.astype(qat_ref.dtype)
        elif j == 2:
            store_heads(ka_ref, 0, [normed_roped(a, gka_ref) for a in heads[:N_KV_A]])
            store_values_t(vat_ref, 0, heads[N_KV_A:])
        elif j < 5:
            za_ref[:, (j - 3) * TN:(j - 2) * TN] = _silu(acc)
        elif j < 7:
            for h, a in enumerate(heads):
                q = _head_norm(a, gqb_ref[...]) * Q_SCALE
                qbt_ref[(j - 5) * heads_per_tile + h] = q.T.astype(qbt_ref.dtype)
        elif j < 9:
            store_heads(kb_ref, (j - 7) * heads_per_tile,
                        [_head_norm(a, gkb_ref[...]) for a in heads])
        elif j < 11:
            store_values_t(vbt_ref, (j - 9) * heads_per_tile, heads)
        else:
            zb_ref[:, (j - 11) * TN:(j - 10) * TN] = _silu(acc)

    acc = column_tile(0)
    for j in range(N_COL_TILES):
        nxt = column_tile(j + 1) if j + 1 < N_COL_TILES else None
        epilogue(j, acc)
        acc = nxt


def _in_projection(x2d, seq, tm, chunk, norm_w, w_in_bf16, gains, tabs):
    m = x2d.shape[0]
    n_tab = seq // tm
    sub = chunk // tm
    cos, sin_a, sin_b = tabs
    row = lambda i: (i, 0)
    const = lambda i: (0, 0)
    bf, f32 = jnp.bfloat16, jnp.float32
    out_shape = [
        jax.ShapeDtypeStruct((N_HEADS_A, HEAD_DIM, m), bf), jax.ShapeDtypeStruct((m, D_KV_A), bf),
        jax.ShapeDtypeStruct((N_KV_A, m // chunk, VT_ROWS, chunk), bf),
        jax.ShapeDtypeStruct((m, D_A), f32),
        jax.ShapeDtypeStruct((N_HEADS_B, HEAD_DIM, m), bf), jax.ShapeDtypeStruct((m, D_B), bf),
        jax.ShapeDtypeStruct((N_HEADS_B, m // tm, VT_ROWS, tm), bf),
        jax.ShapeDtypeStruct((m, D_B), f32),
    ]
    out_specs = [
        pl.BlockSpec((N_HEADS_A, HEAD_DIM, tm), lambda i: (0, 0, i)),
        pl.BlockSpec((tm, D_KV_A), row),
        pl.BlockSpec((N_KV_A, 1, VT_ROWS, tm), lambda i: (0, i // sub, 0, i % sub)),
        pl.BlockSpec((tm, D_A), row),
        pl.BlockSpec((N_HEADS_B, HEAD_DIM, tm), lambda i: (0, 0, i)),
        pl.BlockSpec((tm, D_B), row),
        pl.BlockSpec((N_HEADS_B, 1, VT_ROWS, tm), lambda i: (0, i, 0, 0)),
        pl.BlockSpec((tm, D_B), row),
    ]
    gain_spec = pl.BlockSpec((1, HEAD_DIM), const)
    tab_spec = pl.BlockSpec((tm, HEAD_DIM), lambda i: (i % n_tab, 0))
    return pl.pallas_call(
        _inproj_kernel,
        grid=(m // tm,),
        in_specs=[
            pl.BlockSpec((tm, D_MODEL), row),
            pl.BlockSpec((1, D_MODEL), const),
            pl.BlockSpec((D_MODEL, D_IN), const, pipeline_mode=pl.Buffered(1)),
            gain_spec, gain_spec, gain_spec, gain_spec,
            tab_spec, tab_spec, tab_spec,
        ],
        out_specs=out_specs,
        out_shape=out_shape,
        scratch_shapes=[pltpu.VMEM((tm, D_MODEL), bf)],
        name="in_projection",
        compiler_params=pltpu.CompilerParams(
            dimension_semantics=("arbitrary",),
            vmem_limit_bytes=VMEM_LIMIT),
    )(x2d, norm_w, w_in_bf16, *gains, cos, sin_a, sin_b)


def _gqa_kernel(safe_ref, qt_ref, k_ref, vt_ref, km_ref, vmt_ref, g_ref, o_ref, acc_ref, s_ref):
    tq = qt_ref.shape[2]
    n_chunks, chunk = vt_ref.shape[1], vt_ref.shape[3]
    width = acc_ref.shape[2]
    n_slabs = acc_ref.shape[0]
    heads_per_slab = width // tq
    km = km_ref[:N_META, :]
    vmt = vmt_ref[0, 0]
    meta_pad = jnp.zeros((META_PAD - N_META, width), jnp.bfloat16)
    qts = [jnp.concatenate([qt_ref[i * heads_per_slab + j] for j in range(heads_per_slab)], axis=1)
           if heads_per_slab > 1 else qt_ref[i] for i in range(n_slabs)]

    def meta_values(p):
        return _dot(vmt, jnp.concatenate([p.astype(jnp.bfloat16), meta_pad], axis=0))

    def keys(c):
        if isinstance(c, int):
            return k_ref[c * chunk:(c + 1) * chunk, :]
        return k_ref[pl.ds(pl.multiple_of(c * chunk, chunk), chunk), :]

    def attend(init, consume):
        def chunk_stages(c, state, has_next):
            state = list(state)
            k = keys(c)
            vt = vt_ref[0, c]
            for i in range(n_slabs):
                if i + 1 < n_slabs:
                    s_ref[(i + 1) % 2] = _dot(k, qts[i + 1])
                elif has_next:
                    s_ref[0] = _dot(keys(c + 1), qts[0])
                state[i] = consume(i, s_ref[i % 2], vt, state[i])
            return tuple(state)

        s_ref[0] = _dot(keys(0), qts[0])
        s_meta = [_dot(km, qt) for qt in qts]
        state = tuple(init(i, s) for i, s in enumerate(s_meta))
        state = chunk_stages(0, state, n_chunks > 1)
        if n_chunks > 1:
            state = lax.fori_loop(1, n_chunks - 1, lambda c, st: chunk_stages(c, st, True), state)
            chunk_stages(n_chunks - 1, state, False)
        for i in range(n_slabs):
            acc = acc_ref[i]
            o_slab = acc[:HEAD_DIM] * (1.0 / acc[HEAD_DIM:HEAD_DIM + 1])
            for j in range(heads_per_slab):
                h = i * heads_per_slab + j
                sl = slice(h * HEAD_DIM, (h + 1) * HEAD_DIM)
                o = o_slab[:, j * tq:(j + 1) * tq].T
                o_ref[:, sl] = (o * g_ref[:, sl]).astype(o_ref.dtype)

    safe = safe_ref[0] != 0

    @pl.when(safe)
    def _():
        def init(i, s):
            acc_ref[i] = meta_values(jnp.exp2(s))
            return 0

        def consume(i, s, vt, state):
            acc_ref[i] += _dot(vt, jnp.exp2(s).astype(jnp.bfloat16))
            return state

        attend(init, consume)

    @pl.when(jnp.logical_not(safe))
    def _():
        def init(i, s):
            m = jnp.max(s, axis=0, keepdims=True)
            acc_ref[i] = meta_values(jnp.exp2(s - m))
            return m

        def consume(i, s, vt, m):
            m_new = jnp.maximum(m, jnp.max(s, axis=0, keepdims=True))
            p = jnp.exp2(s - m_new)
            acc_ref[i] = jnp.exp2(m - m_new) * acc_ref[i] + _dot(vt, p.astype(jnp.bfloat16))
            return m_new

        attend(init, consume)


def _global_attention(safe, qat, ka, vat, gate, km, vmt, batch, seq, tq):
    m = gate.shape[0]
    nq = seq // tq
    chunk = vat.shape[3]
    n_chunks = seq // chunk
    qmap = lambda b, g, i: (b * nq + i, g)
    width = GROUP_A * HEAD_DIM
    return pl.pallas_call(
        _gqa_kernel,
        grid=(batch, N_KV_A, nq),
        in_specs=[
            pl.BlockSpec(memory_space=pltpu.SMEM),
            pl.BlockSpec((GROUP_A, HEAD_DIM, tq), lambda b, g, i: (g, 0, b * nq + i)),
            pl.BlockSpec((seq, HEAD_DIM), lambda b, g, i: (b, g)),
            pl.BlockSpec((1, n_chunks, VT_ROWS, chunk), lambda b, g, i: (g, b, 0, 0)),
            pl.BlockSpec((META_PAD, HEAD_DIM), lambda b, g, i: (0, g)),
            pl.BlockSpec((1, 1, VT_ROWS, META_PAD), lambda b, g, i: (g, 0, 0, 0)),
            pl.BlockSpec((tq, width), qmap),
        ],
        out_specs=pl.BlockSpec((tq, width), qmap),
        out_shape=jax.ShapeDtypeStruct((m, D_A), jnp.bfloat16),
        scratch_shapes=[pltpu.VMEM((GROUP_A * tq // GQA_SLAB_WIDTH, VT_ROWS, GQA_SLAB_WIDTH), jnp.float32),
                        pltpu.VMEM((2, chunk, GQA_SLAB_WIDTH), jnp.float32)],
        name="global_attention",
        compiler_params=pltpu.CompilerParams(
            dimension_semantics=("arbitrary", "arbitrary", "arbitrary"),
            vmem_limit_bytes=VMEM_LIMIT),
    )(safe, qat, ka, vat, km, vmt, gate)


def _nbr_kernel(safe_ref, qt_ref, k_ref, vt_ref, km_ref, vmt_ref, bias_ref, g_ref, o_ref, *, n_tiles):
    step = pl.program_id(2)
    bf = jnp.bfloat16
    km = km_ref[:N_META, :]
    vmt = vmt_ref[0, 0]
    meta_pad = jnp.zeros((META_PAD - N_META, NB_Q), bf)
    chunks_per_window = NB_K // NB_VCHUNK

    def run(probs_fn):
        tiles = []
        for u in range(NB_TILES_PER_STEP):
            t = step * NB_TILES_PER_STEP + u
            cls = jnp.where(t == 0, 0, jnp.where(t == n_tiles - 1, 2, 1))
            first_chunk = jnp.clip(t - 1, 0, n_tiles - chunks_per_window)
            tiles.append((u, cls, first_chunk))
        scores = []
        for u, cls, first_chunk in tiles:
            qt = qt_ref[0, :, u * NB_Q:(u + 1) * NB_Q]
            kstart = pl.multiple_of(first_chunk * NB_VCHUNK, NB_VCHUNK)
            s_win = _dot(k_ref[pl.ds(kstart, NB_K), :], qt) + bias_ref[0, cls]
            scores.append((s_win, _dot(km, qt)))
        probs = [probs_fn(s_win, s_meta) for s_win, s_meta in scores]
        for (u, cls, first_chunk), (p_win, p_meta) in zip(tiles, probs):
            acc = _dot(vmt, jnp.concatenate([p_meta, meta_pad], axis=0))
            for j in range(chunks_per_window):
                acc += _dot(vt_ref[0, first_chunk + j], p_win[j * NB_VCHUNK:(j + 1) * NB_VCHUNK])
            o = (acc[:HEAD_DIM] * (1.0 / acc[HEAD_DIM:HEAD_DIM + 1])).T
            rows = slice(u * NB_Q, (u + 1) * NB_Q)
            o_ref[rows, :] = (o * g_ref[rows, :]).astype(o_ref.dtype)

    def probs_safe(s_win, s_meta):
        return jnp.exp2(s_win).astype(bf), jnp.exp2(s_meta).astype(bf)

    def probs_max(s_win, s_meta):
        m = jnp.maximum(jnp.max(s_win, axis=0, keepdims=True), jnp.max(s_meta, axis=0, keepdims=True))
        return jnp.exp2(s_win - m).astype(bf), jnp.exp2(s_meta - m).astype(bf)

    safe = safe_ref[0] != 0

    @pl.when(safe)
    def _():
        run(probs_safe)

    @pl.when(jnp.logical_not(safe))
    def _():
        run(probs_max)


def _neighbourhood_attention(safe, qbt, kb, vbt, gate, km, vmt, bias, batch, seq):
    m = gate.shape[0]
    tqb = NB_TILES_PER_STEP * NB_Q
    nq = seq // tqb
    n_vchunks = seq // NB_VCHUNK
    assert vbt.shape[3] == NB_VCHUNK
    qmap = lambda h, b, i: (b * nq + i, h)
    return pl.pallas_call(
        functools.partial(_nbr_kernel, n_tiles=seq // NB_Q),
        grid=(N_HEADS_B, batch, nq),
        in_specs=[
            pl.BlockSpec(memory_space=pltpu.SMEM),
            pl.BlockSpec((1, HEAD_DIM, tqb), lambda h, b, i: (h, 0, b * nq + i)),
            pl.BlockSpec((seq, HEAD_DIM), lambda h, b, i: (b, h)),
            pl.BlockSpec((1, n_vchunks, VT_ROWS, NB_VCHUNK), lambda h, b, i: (h, b, 0, 0)),
            pl.BlockSpec((META_PAD, HEAD_DIM), lambda h, b, i: (0, h)),
            pl.BlockSpec((1, 1, VT_ROWS, META_PAD), lambda h, b, i: (h, 0, 0, 0)),
            pl.BlockSpec((1, NB_CLASSES, NB_K, NB_Q), lambda h, b, i: (h, 0, 0, 0)),
            pl.BlockSpec((tqb, HEAD_DIM), qmap),
        ],
        out_specs=pl.BlockSpec((tqb, HEAD_DIM), qmap),
        out_shape=jax.ShapeDtypeStruct((m, D_B), jnp.bfloat16),
        name="neighbourhood_attention",
        compiler_params=pltpu.CompilerParams(
            dimension_semantics=("arbitrary", "arbitrary", "arbitrary"),
            vmem_limit_bytes=VMEM_LIMIT),
    )(safe, qbt, kb, vbt, km, vmt, bias, gate)


def _outproj_kernel(x_ref, ya_ref, yb_ref, w_ref, o_ref):
    y = _dot(ya_ref[...], w_ref[:D_A, :]) + _dot(yb_ref[...], w_ref[D_A:, :])
    o_ref[...] = x_ref[...] + y


def _out_projection(x2d, ya, yb, w_out_bf16, tm):
    m = x2d.shape[0]
    row = lambda i: (i, 0)
    return pl.pallas_call(
        _outproj_kernel,
        grid=(m // tm,),
        in_specs=[
            pl.BlockSpec((tm, D_MODEL), row),
            pl.BlockSpec((tm, D_A), row),
            pl.BlockSpec((tm, D_B), row),
            pl.BlockSpec((D_A + D_B, D_MODEL), lambda i: (0, 0)),
        ],
        out_specs=pl.BlockSpec((tm, D_MODEL), row),
        out_shape=jax.ShapeDtypeStruct((m, D_MODEL), jnp.float32),
        name="out_projection",
        compiler_params=pltpu.CompilerParams(
            dimension_semantics=("arbitrary",),
            vmem_limit_bytes=VMEM_LIMIT),
    )(x2d, ya, yb, w_out_bf16)


def _rope_tables(rows, cols):
    half = HEAD_DIM // 2
    inv_freq = ROPE_THETA ** (-jnp.arange(0, half, 2, dtype=jnp.float32) / half)
    ang_r = rows.astype(jnp.float32)[:, None] * inv_freq[None, :]
    ang_c = cols.astype(jnp.float32)[:, None] * inv_freq[None, :]
    n_r, n_c = rows.shape[0], cols.shape[0]

    def grid(r_part, c_part):
        r_part = jnp.broadcast_to(r_part[:, None, :], (n_r, n_c, r_part.shape[-1]))
        c_part = jnp.broadcast_to(c_part[None, :, :], (n_r, n_c, c_part.shape[-1]))
        return jnp.concatenate([r_part, c_part], axis=-1).reshape(n_r * n_c, HEAD_DIM)

    cr, sr, cc, sc = jnp.cos(ang_r), jnp.sin(ang_r), jnp.cos(ang_c), jnp.sin(ang_c)
    zr, zc = jnp.zeros_like(sr), jnp.zeros_like(sc)
    cos = grid(jnp.concatenate([cr, cr], -1), jnp.concatenate([cc, cc], -1))
    sin_a = grid(jnp.concatenate([-sr, zr], -1), jnp.concatenate([-sc, zc], -1))
    sin_b = grid(jnp.concatenate([zr, sr], -1), jnp.concatenate([zc, sc], -1))
    return cos, sin_a, sin_b


def _nbr_bias_tables(rpb):
    rows = 32
    tile_r = np.array([0, 2 * NB_QROWS, rows - NB_QROWS])[:, None, None]
    dr = np.arange(NB_QROWS)[None, :, None]
    jj = np.arange(NB_KROWS)[None, None, :]
    k0 = np.clip(tile_r - NB_QROWS, 0, rows - NB_KROWS)
    r = tile_r + dr
    kr = k0 + jj
    r0 = np.clip(r - WIN_R // 2, 0, rows - WIN_R)
    row_valid = (kr >= r0) & (kr < r0 + WIN_R)
    off_r = np.clip(kr - r + (WIN_R - 1), 0, 2 * WIN_R - 2)
    n_off_c = 2 * WIN_C - 1
    pos = np.arange(GRID_W)
    onehot = (pos[None, :, None] - pos[None, None, :] + (WIN_C - 1)
              == np.arange(n_off_c)[:, None, None]).astype(np.float32)
    toeplitz = jnp.einsum('hro,okc->hrkc', rpb.astype(jnp.float32) * LOG2E, onehot,
                          precision=lax.Precision.HIGHEST)

    def assemble(toep_ref, out_ref):
        kc = lax.broadcasted_iota(jnp.int32, (GRID_W, GRID_W), 0)
        c = lax.broadcasted_iota(jnp.int32, (GRID_W, GRID_W), 1)
        c0 = jnp.clip(c - WIN_C // 2, 0, GRID_W - WIN_C)
        col_valid = (kc >= c0) & (kc < c0 + WIN_C)
        masked = jnp.full((GRID_W, GRID_W), MASK_VALUE, jnp.float32)
        for cls in range(NB_CLASSES):
            for j in range(NB_KROWS):
                blocks = [jnp.where(col_valid, toep_ref[0, int(off_r[cls, d, j])], MASK_VALUE)
                          if row_valid[cls, d, j] else masked for d in range(NB_QROWS)]
                out_ref[0, cls, j * GRID_W:(j + 1) * GRID_W, :] = jnp.concatenate(blocks, axis=1)

    return pl.pallas_call(
        assemble,
        grid=(N_HEADS_B,),
        in_specs=[pl.BlockSpec((1, 2 * WIN_R - 1, GRID_W, GRID_W), lambda h: (h, 0, 0, 0))],
        out_specs=pl.BlockSpec((1, NB_CLASSES, NB_K, NB_Q), lambda h: (h, 0, 0, 0)),
        out_shape=jax.ShapeDtypeStruct((N_HEADS_B, NB_CLASSES, NB_K, NB_Q), jnp.float32),
        name="nbr_bias_table",
    )(toeplitz)


def kernel(x_prompt, x_sample, meta_tokens, norm_w, w_in, q_norm_a, k_norm_a, q_norm_b, k_norm_b, rpb, w_out):
    w_in_bf = w_in[0].astype(jnp.bfloat16)
    w_out_bf = w_out[0].astype(jnp.bfloat16)
    nw = norm_w[0].reshape(1, D_MODEL)
    gains = [g[0].reshape(1, HEAD_DIM) for g in (q_norm_a, k_norm_a, q_norm_b, k_norm_b)]
    bias = _nbr_bias_tables(rpb[0])

    score_bound_a = (HEAD_DIM * Q_SCALE * jnp.max(jnp.abs(q_norm_a[0])) * jnp.max(jnp.abs(k_norm_a[0])))
    safe_a = (score_bound_a <= SAFE_SCORE_BOUND).astype(jnp.int32).reshape(1)
    score_bound_b = (HEAD_DIM * Q_SCALE * jnp.max(jnp.abs(q_norm_b[0])) * jnp.max(jnp.abs(k_norm_b[0]))
                     + LOG2E * jnp.max(jnp.abs(rpb[0])))
    safe_b = (score_bound_b <= SAFE_SCORE_BOUND).astype(jnp.int32).reshape(1)

    meta_tabs = _rope_tables(jnp.full((1,), -1, jnp.int32), jnp.arange(META_PAD, dtype=jnp.int32))
    meta_x = jnp.pad(meta_tokens, ((0, META_PAD - N_META), (0, 0)))
    meta_out = _in_projection(meta_x, META_PAD, META_PAD, META_PAD, nw, w_in_bf, gains, meta_tabs)
    _, km_a, vmt_a, _, _, km_b, vmt_b, _ = meta_out

    max_seq = max(x_prompt.shape[1], x_sample.shape[1])
    tabs = _rope_tables(jnp.arange(max_seq // GRID_W, dtype=jnp.int32),
                        jnp.arange(GRID_W, dtype=jnp.int32))

    def encode(x):
        batch, seq, _ = x.shape
        x2d = x.reshape(batch * seq, D_MODEL)
        qat, ka, vat, ga, qbt, kb, vbt, gb = _in_projection(
            x2d, seq, IN_ROW_TILE, GQA_KEY_CHUNK, nw, w_in_bf, gains, tabs)
        ya = _global_attention(safe_a, qat, ka, vat, ga, km_a, vmt_a, batch, seq, GQA_Q_TILE)
        yb = _neighbourhood_attention(safe_b, qbt, kb, vbt, gb, km_b, vmt_b, bias, batch, seq)
        y = _out_projection(x2d, ya, yb, w_out_bf, OUT_ROW_TILE)
        return y.reshape(batch, seq, D_MODEL)

    return (encode(x_prompt), encode(x_sample))
```

```python
import functools
import math

import jax
import jax.numpy as jnp
import numpy as np
from jax import lax
from jax.experimental import pallas as pl
from jax.experimental.pallas import tpu as pltpu

D_MODEL = 2048
HEAD_DIM = 128
N_HEADS_A = 8
N_KV_A = 2
GROUP_A = N_HEADS_A // N_KV_A
N_HEADS_B = 8
D_A = N_HEADS_A * HEAD_DIM
D_KV_A = N_KV_A * HEAD_DIM
D_B = N_HEADS_B * HEAD_DIM
D_IN = D_A + 2 * D_KV_A + D_A + 4 * D_B
N_META = 16
GRID_W = 64
WIN_R = 8
WIN_C = 16
ROPE_THETA = 10000.0
EPS = 1e-6

LOG2E = math.log2(math.e)
Q_SCALE = HEAD_DIM ** -0.5 * LOG2E
MASK_VALUE = -1e30

LANES = 128
BF16_SUBLANE_TILE = 16
VT_EXTRA_ROWS = BF16_SUBLANE_TILE
VT_ROWS = HEAD_DIM + VT_EXTRA_ROWS
SAFE_SCORE_BOUND = 60.0
META_PAD = LANES
VMEM_LIMIT = 56 * 1024 * 1024

TN = 512
N_COL_TILES = D_IN // TN
NB_QROWS = 4
NB_Q = NB_QROWS * GRID_W
NB_KROWS = 3 * NB_QROWS
NB_K = NB_KROWS * GRID_W
NB_CLASSES = 3

IN_ROW_TILE = 256
OUT_ROW_TILE = 512
GQA_KEY_CHUNK = 1024
GQA_Q_TILE = 1024
GQA_SLAB_WIDTH = 1024
NB_TILES_PER_STEP = 8
NB_VCHUNK = IN_ROW_TILE


def _silu(z):
    return z * (1.0 / (1.0 + jnp.exp(-z)))


def _head_norm(a, gain):
    ms = jnp.mean(a * a, axis=-1, keepdims=True)
    return a * lax.rsqrt(ms + EPS) * gain


def _rope(y, cos, sin_a, sin_b):
    return y * cos + pltpu.roll(y, 96, 1) * sin_a + pltpu.roll(y, 32, 1) * sin_b


def _dot(a, b):
    return jnp.dot(a, b, preferred_element_type=jnp.float32)


def _inproj_kernel(x_ref, nw_ref, w_ref, gqa_ref, gka_ref, gqb_ref, gkb_ref,
                   cos_ref, sa_ref, sb_ref,
                   qat_ref, ka_ref, vat_ref, za_ref, qbt_ref, kb_ref, vbt_ref, zb_ref,
                   xn_ref):
    x = x_ref[...]
    ms = jnp.mean(x * x, axis=-1, keepdims=True)
    xn_ref[...] = (x * lax.rsqrt(ms + EPS) * nw_ref[...]).astype(jnp.bfloat16)

    def column_tile(j):
        return _dot(xn_ref[...], w_ref[:, j * TN:(j + 1) * TN])

    def normed_roped(a, gain_ref):
        return _rope(_head_norm(a, gain_ref[...]), cos_ref[...], sa_ref[...], sb_ref[...])

    def store_heads(ref, first, vals):
        for h, v in enumerate(vals):
            ref[:, (first + h) * HEAD_DIM:(first + h + 1) * HEAD_DIM] = v.astype(ref.dtype)

    heads_per_tile = TN // HEAD_DIM

    def store_values_t(ref, first, vals):
        ones_row = (lax.broadcasted_iota(jnp.int32, (VT_EXTRA_ROWS, vals[0].shape[0]), 0) == 0)
        for h, a in enumerate(vals):
            ref[first + h, 0, :HEAD_DIM, :] = a.T.astype(ref.dtype)
            ref[first + h, 0, HEAD_DIM:, :] = ones_row.astype(ref.dtype)

    def epilogue(j, acc):
        heads = [acc[:, h * HEAD_DIM:(h + 1) * HEAD_DIM] for h in range(heads_per_tile)]
        if j < 2:
            for h, a in enumerate(heads):
                q = normed_roped(a, gqa_ref) * Q_SCALE
                qat_ref[j * heads_per_tile + h] = q.T.astype(qat_ref.dtype)
        elif j == 2:
            store_heads(ka_ref, 0, [normed_roped(a, gka_ref) for a in heads[:N_KV_A]])
            store_values_t(vat_ref, 0, heads[N_KV_A:])
        elif j < 5:
            za_ref[:, (j - 3) * TN:(j - 2) * TN] = _silu(acc)
        elif j < 7:
            for h, a in enumerate(heads):
                q = _head_norm(a, gqb_ref[...]) * Q_SCALE
                qbt_ref[(j - 5) * heads_per_tile + h] = q.T.astype(qbt_ref.dtype)
        elif j < 9:
            store_heads(kb_ref, (j - 7) * heads_per_tile,
                        [_head_norm(a, gkb_ref[...]) for a in heads])
        elif j < 11:
            store_values_t(vbt_ref, (j - 9) * heads_per_tile, heads)
        else:
            zb_ref[:, (j - 11) * TN:(j - 10) * TN] = _silu(acc)

    acc = column_tile(0)
    for j in range(N_COL_TILES):
        nxt = column_tile(j + 1) if j + 1 < N_COL_TILES else None
        epilogue(j, acc)
        acc = nxt


def _in_projection(x2d, seq, tm, chunk, norm_w, w_in_bf16, gains, tabs):
    m = x2d.shape[0]
    n_tab = seq // tm
    sub = chunk // tm
    cos, sin_a, sin_b = tabs
    row = lambda i: (i, 0)
    const = lambda i: (0, 0)
    bf, f32 = jnp.bfloat16, jnp.float32
    out_shape = [
        jax.ShapeDtypeStruct((N_HEADS_A, HEAD_DIM, m), bf), jax.ShapeDtypeStruct((m, D_KV_A), bf),
        jax.ShapeDtypeStruct((N_KV_A, m // chunk, VT_ROWS, chunk), bf),
        jax.ShapeDtypeStruct((m, D_A), f32),
        jax.ShapeDtypeStruct((N_HEADS_B, HEAD_DIM, m), bf), jax.ShapeDtypeStruct((m, D_B), bf),
        jax.ShapeDtypeStruct((N_HEADS_B, m // tm, VT_ROWS, tm), bf),
        jax.ShapeDtypeStruct((m, D_B), f32),
    ]
    out_specs = [
        pl.BlockSpec((N_HEADS_A, HEAD_DIM, tm), lambda i: (0, 0, i)),
        pl.BlockSpec((tm, D_KV_A), row),
        pl.BlockSpec((N_KV_A, 1, VT_ROWS, tm), lambda i: (0, i // sub, 0, i % sub)),
        pl.BlockSpec((tm, D_A), row),
        pl.BlockSpec((N_HEADS_B, HEAD_DIM, tm), lambda i: (0, 0, i)),
        pl.BlockSpec((tm, D_B), row),
        pl.BlockSpec((N_HEADS_B, 1, VT_ROWS, tm), lambda i: (0, i, 0, 0)),
        pl.BlockSpec((tm, D_B), row),
    ]
    gain_spec = pl.BlockSpec((1, HEAD_DIM), const)
    tab_spec = pl.BlockSpec((tm, HEAD_DIM), lambda i: (i % n_tab, 0))
    return pl.pallas_call(
        _inproj_kernel,
        grid=(m // tm,),
        in_specs=[
            pl.BlockSpec((tm, D_MODEL), row),
            pl.BlockSpec((1, D_MODEL), const),
            pl.BlockSpec((D_MODEL, D_IN), const, pipeline_mode=pl.Buffered(1)),
            gain_spec, gain_spec, gain_spec, gain_spec,
            tab_spec, tab_spec, tab_spec,
        ],
        out_specs=out_specs,
        out_shape=out_shape,
        scratch_shapes=[pltpu.VMEM((tm, D_MODEL), bf)],
        name="in_projection",
        compiler_params=pltpu.CompilerParams(
            dimension_semantics=("arbitrary",),
            vmem_limit_bytes=VMEM_LIMIT),
    )(x2d, norm_w, w_in_bf16, *gains, cos, sin_a, sin_b)


def _gqa_kernel(safe_ref, qt_ref, k_ref, vt_ref, km_ref, vmt_ref, g_ref, o_ref, acc_ref, s_ref):
    tq = qt_ref.shape[2]
    n_chunks, chunk = vt_ref.shape[1], vt_ref.shape[3]
    width = acc_ref.shape[2]
    n_slabs = acc_ref.shape[0]
    heads_per_slab = width // tq
    km = km_ref[:N_META, :]
    vmt = vmt_ref[0, 0]
    meta_pad = jnp.zeros((META_PAD - N_META, width), jnp.bfloat16)
    qts = [jnp.concatenate([qt_ref[i * heads_per_slab + j] for j in range(heads_per_slab)], axis=1)
           if heads_per_slab > 1 else qt_ref[i] for i in range(n_slabs)]

    def meta_values(p):
        return _dot(vmt, jnp.concatenate([p.astype(jnp.bfloat16), meta_pad], axis=0))

    def keys(c):
        if isinstance(c, int):
            return k_ref[c * chunk:(c + 1) * chunk, :]
        return k_ref[pl.ds(pl.multiple_of(c * chunk, chunk), chunk), :]

    def attend(init, consume):
        def chunk_stages(c, state, has_next):
            state = list(state)
            k = keys(c)
            vt = vt_ref[0, c]
            for i in range(n_slabs):
                if i + 1 < n_slabs:
                    s_ref[(i + 1) % 2] = _dot(k, qts[i + 1])
                elif has_next:
                    s_ref[0] = _dot(keys(c + 1), qts[0])
                state[i] = consume(i, s_ref[i % 2], vt, state[i])
            return tuple(state)

        s_ref[0] = _dot(keys(0), qts[0])
        s_meta = [_dot(km, qt) for qt in qts]
        state = tuple(init(i, s) for i, s in enumerate(s_meta))
        state = chunk_stages(0, state, n_chunks > 1)
        if n_chunks > 1:
            state = lax.fori_loop(1, n_chunks - 1, lambda c, st: chunk_stages(c, st, True), state)
            chunk_stages(n_chunks - 1, state, False)
        for i in range(n_slabs):
            acc = acc_ref[i]
            o_slab = acc[:HEAD_DIM] * (1.0 / acc[HEAD_DIM:HEAD_DIM + 1])
            for j in range(heads_per_slab):
                h = i * heads_per_slab + j
                sl = slice(h * HEAD_DIM, (h + 1) * HEAD_DIM)
                o = o_slab[:, j * tq:(j + 1) * tq].T
                o_ref[:, sl] = (o * g_ref[:, sl]).astype(o_ref.dtype)

    safe = safe_ref[0] != 0

    @pl.when(safe)
    def _():
        def init(i, s):
            acc_ref[i] = meta_values(jnp.exp2(s))
            return 0

        def consume(i, s, vt, state):
            acc_ref[i] += _dot(vt, jnp.exp2(s).astype(jnp.bfloat16))
            return state

        attend(init, consume)

    @pl.when(jnp.logical_not(safe))
    def _():
        def init(i, s):
            m = jnp.max(s, axis=0, keepdims=True)
            acc_ref[i] = meta_values(jnp.exp2(s - m))
            return m

        def consume(i, s, vt, m):
            m_new = jnp.maximum(m, jnp.max(s, axis=0, keepdims=True))
            p = jnp.exp2(s - m_new)
            acc_ref[i] = jnp.exp2(m - m_new) * acc_ref[i] + _dot(vt, p.astype(jnp.bfloat16))
            return m_new

        attend(init, consume)


def _global_attention(safe, qat, ka, vat, gate, km, vmt, batch, seq, tq):
    m = gate.shape[0]
    nq = seq // tq
    chunk = vat.shape[3]
    n_chunks = seq // chunk
    qmap = lambda b, g, i: (b * nq + i, g)
    width = GROUP_A * HEAD_DIM
    return pl.pallas_call(
        _gqa_kernel,
        grid=(batch, N_KV_A, nq),
        in_specs=[
            pl.BlockSpec(memory_space=pltpu.SMEM),
            pl.BlockSpec((GROUP_A, HEAD_DIM, tq), lambda b, g, i: (g, 0, b * nq + i)),
            pl.BlockSpec((seq, HEAD_DIM), lambda b, g, i: (b, g)),
            pl.BlockSpec((1, n_chunks, VT_ROWS, chunk), lambda b, g, i: (g, b, 0, 0)),
            pl.BlockSpec((META_PAD, HEAD_DIM), lambda b, g, i: (0, g)),
            pl.BlockSpec((1, 1, VT_ROWS, META_PAD), lambda b, g, i: (g, 0, 0, 0)),
            pl.BlockSpec((tq, width), qmap),
        ],
        out_specs=pl.BlockSpec((tq, width), qmap),
        out_shape=jax.ShapeDtypeStruct((m, D_A), jnp.bfloat16),
        scratch_shapes=[pltpu.VMEM((GROUP_A * tq // GQA_SLAB_WIDTH, VT_ROWS, GQA_SLAB_WIDTH), jnp.float32),
                        pltpu.VMEM((2, chunk, GQA_SLAB_WIDTH), jnp.float32)],
        name="global_attention",
        compiler_params=pltpu.CompilerParams(
            dimension_semantics=("arbitrary", "arbitrary", "arbitrary"),
            vmem_limit_bytes=VMEM_LIMIT),
    )(safe, qat, ka, vat, km, vmt, gate)


def _nbr_kernel(safe_ref, qt_ref, k_ref, vt_ref, km_ref, vmt_ref, bias_ref, g_ref, o_ref, *, n_tiles):
    step = pl.program_id(2)
    bf = jnp.bfloat16
    km = km_ref[:N_META, :]
    vmt = vmt_ref[0, 0]
    meta_pad = jnp.zeros((META_PAD - N_META, NB_Q), bf)
    chunks_per_window = NB_K // NB_VCHUNK

    def run(probs_fn):
        tiles = []
        for u in range(NB_TILES_PER_STEP):
            t = step * NB_TILES_PER_STEP + u
            cls = jnp.where(t == 0, 0, jnp.where(t == n_tiles - 1, 2, 1))
            first_chunk = jnp.clip(t - 1, 0, n_tiles - chunks_per_window)
            tiles.append((u, cls, first_chunk))
        scores = []
        for u, cls, first_chunk in tiles:
            qt = qt_ref[0, :, u * NB_Q:(u + 1) * NB_Q]
            kstart = pl.multiple_of(first_chunk * NB_VCHUNK, NB_VCHUNK)
            s_win = _dot(k_ref[pl.ds(kstart, NB_K), :], qt) + bias_ref[0, cls]
            scores.append((s_win, _dot(km, qt)))
        probs = [probs_fn(s_win, s_meta) for s_win, s_meta in scores]
        for (u, cls, first_chunk), (p_win, p_meta) in zip(tiles, probs):
            acc = _dot(vmt, jnp.concatenate([p_meta, meta_pad], axis=0))
            for j in range(chunks_per_window):
                acc += _dot(vt_ref[0, first_chunk + j], p_win[j * NB_VCHUNK:(j + 1) * NB_VCHUNK])
            o = (acc[:HEAD_DIM] * (1.0 / acc[HEAD_DIM:HEAD_DIM + 1])).T
            rows = slice(u * NB_Q, (u + 1) * NB_Q)
            o_ref[rows, :] = (o * g_ref[rows, :]).astype(o_ref.dtype)

    def probs_safe(s_win, s_meta):
        return jnp.exp2(s_win).astype(bf), jnp.exp2(s_meta).astype(bf)

    def probs_max(s_win, s_meta):
        m = jnp.maximum(jnp.max(s_win, axis=0, keepdims=True), jnp.max(s_meta, axis=0, keepdims=True))
        return jnp.exp2(s_win - m).astype(bf), jnp.exp2(s_meta - m).astype(bf)

    safe = safe_ref[0] != 0

    @pl.when(safe)
    def _():
        run(probs_safe)

    @pl.when(jnp.logical_not(safe))
    def _():
        run(probs_max)


def _neighbourhood_attention(safe, qbt, kb, vbt, gate, km, vmt, bias, batch, seq):
    m = gate.shape[0]
    tqb = NB_TILES_PER_STEP * NB_Q
    nq = seq // tqb
    n_vchunks = seq // NB_VCHUNK
    assert vbt.shape[3] == NB_VCHUNK
    qmap = lambda h, b, i: (b * nq + i, h)
    return pl.pallas_call(
        functools.partial(_nbr_kernel, n_tiles=seq // NB_Q),
        grid=(N_HEADS_B, batch, nq),
        in_specs=[
            pl.BlockSpec(memory_space=pltpu.SMEM),
            pl.BlockSpec((1, HEAD_DIM, tqb), lambda h, b, i: (h, 0, b * nq + i)),
            pl.BlockSpec((seq, HEAD_DIM), lambda h, b, i: (b, h)),
            pl.BlockSpec((1, n_vchunks, VT_ROWS, NB_VCHUNK), lambda h, b, i: (h, b, 0, 0)),
            pl.BlockSpec((META_PAD, HEAD_DIM), lambda h, b, i: (0, h)),
            pl.BlockSpec((1, 1, VT_ROWS, META_PAD), lambda h, b, i: (h, 0, 0, 0)),
            pl.BlockSpec((1, NB_CLASSES, NB_K, NB_Q), lambda h, b, i: (h, 0, 0, 0)),
            pl.BlockSpec((tqb, HEAD_DIM), qmap),
        ],
        out_specs=pl.BlockSpec((tqb, HEAD_DIM), qmap),
        out_shape=jax.ShapeDtypeStruct((m, D_B), jnp.bfloat16),
        name="neighbourhood_attention",
        compiler_params=pltpu.CompilerParams(
            dimension_semantics=("arbitrary", "arbitrary", "arbitrary"),
            vmem_limit_bytes=VMEM_LIMIT),
    )(safe, qbt, kb, vbt, km, vmt, bias, gate)


def _outproj_kernel(x_ref, ya_ref, yb_ref, w_ref, o_ref):
    y = _dot(ya_ref[...], w_ref[:D_A, :]) + _dot(yb_ref[...], w_ref[D_A:, :])
    o_ref[...] = x_ref[...] + y


def _out_projection(x2d, ya, yb, w_out_bf16, tm):
    m = x2d.shape[0]
    row = lambda i: (i, 0)
    return pl.pallas_call(
        _outproj_kernel,
        grid=(m // tm,),
        in_specs=[
            pl.BlockSpec((tm, D_MODEL), row),
            pl.BlockSpec((tm, D_A), row),
            pl.BlockSpec((tm, D_B), row),
            pl.BlockSpec((D_A + D_B, D_MODEL), lambda i: (0, 0)),
        ],
        out_specs=pl.BlockSpec((tm, D_MODEL), row),
        out_shape=jax.ShapeDtypeStruct((m, D_MODEL), jnp.float32),
        name="out_projection",
        compiler_params=pltpu.CompilerParams(
            dimension_semantics=("arbitrary",),
            vmem_limit_bytes=VMEM_LIMIT),
    )(x2d, ya, yb, w_out_bf16)


def _rope_tables(rows, cols):
    half = HEAD_DIM // 2
    inv_freq = ROPE_THETA ** (-jnp.arange(0, half, 2, dtype=jnp.float32) / half)
    ang_r = rows.astype(jnp.float32)[:, None] * inv_freq[None, :]
    ang_c = cols.astype(jnp.float32)[:, None] * inv_freq[None, :]
    n_r, n_c = rows.shape[0], cols.shape[0]

    def grid(r_part, c_part):
        r_part = jnp.broadcast_to(r_part[:, None, :], (n_r, n_c, r_part.shape[-1]))
        c_part = jnp.broadcast_to(c_part[None, :, :], (n_r, n_c, c_part.shape[-1]))
        return jnp.concatenate([r_part, c_part], axis=-1).reshape(n_r * n_c, HEAD_DIM)

    cr, sr, cc, sc = jnp.cos(ang_r), jnp.sin(ang_r), jnp.cos(ang_c), jnp.sin(ang_c)
    zr, zc = jnp.zeros_like(sr), jnp.zeros_like(sc)
    cos = grid(jnp.concatenate([cr, cr], -1), jnp.concatenate([cc, cc], -1))
    sin_a = grid(jnp.concatenate([-sr, zr], -1), jnp.concatenate([-sc, zc], -1))
    sin_b = grid(jnp.concatenate([zr, sr], -1), jnp.concatenate([zc, sc], -1))
    return cos, sin_a, sin_b


def _nbr_bias_tables(rpb):
    rows = 32
    tile_r = np.array([0, 2 * NB_QROWS, rows - NB_QROWS])[:, None, None]
    dr = np.arange(NB_QROWS)[None, :, None]
    jj = np.arange(NB_KROWS)[None, None, :]
    k0 = np.clip(tile_r - NB_QROWS, 0, rows - NB_KROWS)
    r = tile_r + dr
    kr = k0 + jj
    r0 = np.clip(r - WIN_R // 2, 0, rows - WIN_R)
    row_valid = (kr >= r0) & (kr < r0 + WIN_R)
    off_r = np.clip(kr - r + (WIN_R - 1), 0, 2 * WIN_R - 2)
    n_off_c = 2 * WIN_C - 1
    pos = np.arange(GRID_W)
    onehot = (pos[None, :, None] - pos[None, None, :] + (WIN_C - 1)
              == np.arange(n_off_c)[:, None, None]).astype(np.float32)
    toeplitz = jnp.einsum('hro,okc->hrkc', rpb.astype(jnp.float32) * LOG2E, onehot,
                          precision=lax.Precision.HIGHEST)

    def assemble(toep_ref, out_ref):
        kc = lax.broadcasted_iota(jnp.int32, (GRID_W, GRID_W), 0)
        c = lax.broadcasted_iota(jnp.int32, (GRID_W, GRID_W), 1)
        c0 = jnp.clip(c - WIN_C // 2, 0, GRID_W - WIN_C)
        col_valid = (kc >= c0) & (kc < c0 + WIN_C)
        masked = jnp.full((GRID_W, GRID_W), MASK_VALUE, jnp.float32)
        for cls in range(NB_CLASSES):
            for j in range(NB_KROWS):
                blocks = [jnp.where(col_valid, toep_ref[0, int(off_r[cls, d, j])], MASK_VALUE)
                          if row_valid[cls, d, j] else masked for d in range(NB_QROWS)]
                out_ref[0, cls, j * GRID_W:(j + 1) * GRID_W, :] = jnp.concatenate(blocks, axis=1)

    return pl.pallas_call(
        assemble,
        grid=(N_HEADS_B,),
        in_specs=[pl.BlockSpec((1, 2 * WIN_R - 1, GRID_W, GRID_W), lambda h: (h, 0, 0, 0))],
        out_specs=pl.BlockSpec((1, NB_CLASSES, NB_K, NB_Q), lambda h: (h, 0, 0, 0)),
        out_shape=jax.ShapeDtypeStruct((N_HEADS_B, NB_CLASSES, NB_K, NB_Q), jnp.float32),
        name="nbr_bias_table",
    )(toeplitz)


def kernel(x_prompt, x_sample, meta_tokens, norm_w, w_in, q_norm_a, k_norm_a, q_norm_b, k_norm_b, rpb, w_out):
    w_in_bf = w_in[0].astype(jnp.bfloat16)
    w_out_bf = w_out[0].astype(jnp.bfloat16)
    nw = norm_w[0].reshape(1, D_MODEL)
    gains = [g[0].reshape(1, HEAD_DIM) for g in (q_norm_a, k_norm_a, q_norm_b, k_norm_b)]
    bias = _nbr_bias_tables(rpb[0])

    score_bound_a = (HEAD_DIM * Q_SCALE * jnp.max(jnp.abs(q_norm_a[0])) * jnp.max(jnp.abs(k_norm_a[0])))
    safe_a = (score_bound_a <= SAFE_SCORE_BOUND).astype(jnp.int32).reshape(1)
    score_bound_b = (HEAD_DIM * Q_SCALE * jnp.max(jnp.abs(q_norm_b[0])) * jnp.max(jnp.abs(k_norm_b[0]))
                     + LOG2E * jnp.max(jnp.abs(rpb[0])))
    safe_b = (score_bound_b <= SAFE_SCORE_BOUND).astype(jnp.int32).reshape(1)

    meta_tabs = _rope_tables(jnp.full((1,), -1, jnp.int32), jnp.arange(META_PAD, dtype=jnp.int32))
    meta_x = jnp.pad(meta_tokens, ((0, META_PAD - N_META), (0, 0)))
    meta_out = _in_projection(meta_x, META_PAD, META_PAD, META_PAD, nw, w_in_bf, gains, meta_tabs)
    _, km_a, vmt_a, _, _, km_b, vmt_b, _ = meta_out

    max_seq = max(x_prompt.shape[1], x_sample.shape[1])
    tabs = _rope_tables(jnp.arange(max_seq // GRID_W, dtype=jnp.int32),
                        jnp.arange(GRID_W, dtype=jnp.int32))

    def encode(x):
        batch, seq, _ = x.shape
        x2d = x.reshape(batch * seq, D_MODEL)
        qat, ka, vat, ga, qbt, kb, vbt, gb = _in_projection(
            x2d, seq, IN_ROW_TILE, GQA_KEY_CHUNK, nw, w_in_bf, gains, tabs)
        ya = _global_attention(safe_a, qat, ka, vat, ga, km_a, vmt_a, batch, seq, GQA_Q_TILE)
        yb = _neighbourhood_attention(safe_b, qbt, kb, vbt, gb, km_b, vmt_b, bias, batch, seq)
        y = _out_projection(x2d, ya, yb, w_out_bf, OUT_ROW_TILE)
        return y.reshape(batch, seq, D_MODEL)

    return (encode(x_prompt), encode(x_sample))
```

```python
import functools
import math

import jax
import jax.numpy as jnp
import numpy as np
from jax import lax
from jax.experimental import pallas as pl
from jax.experimental.pallas import tpu as pltpu

D_MODEL = 2048
HEAD_DIM = 128
N_HEADS_A = 8
N_KV_A = 2
GROUP_A = N_HEADS_A // N_KV_A
N_HEADS_B = 8
D_A = N_HEADS_A * HEAD_DIM
D_KV_A = N_KV_A * HEAD_DIM
D_B = N_HEADS_B * HEAD_DIM
D_IN = D_A + 2 * D_KV_A + D_A + 4 * D_B
N_META = 16
GRID_W = 64
WIN_R = 8
WIN_C = 16
ROPE_THETA = 10000.0
EPS = 1e-6

LOG2E = math.log2(math.e)
Q_SCALE = HEAD_DIM ** -0.5 * LOG2E
MASK_VALUE = -1e30

LANES = 128
BF16_SUBLANE_TILE = 16
VT_EXTRA_ROWS = BF16_SUBLANE_TILE
VT_ROWS = HEAD_DIM + VT_EXTRA_ROWS
SAFE_SCORE_BOUND = 60.0
META_PAD = LANES
VMEM_LIMIT = 56 * 1024 * 1024

TN = 512
N_COL_TILES = D_IN // TN
NB_QROWS = 4
NB_Q = NB_QROWS * GRID_W
NB_KROWS = 3 * NB_QROWS
NB_K = NB_KROWS * GRID_W
NB_CLASSES = 3

IN_ROW_TILE = 256
OUT_ROW_TILE = 512
GQA_KEY_CHUNK = 1024
GQA_Q_TILE = 1024
GQA_SLAB_WIDTH = 1024
NB_TILES_PER_STEP = 16
NB_VCHUNK = IN_ROW_TILE


def _silu(z):
    return z * (1.0 / (1.0 + jnp.exp(-z)))


def _head_norm(a, gain):
    ms = jnp.mean(a * a, axis=-1, keepdims=True)
    return a * lax.rsqrt(ms + EPS) * gain


def _rope(y, cos, sin_a, sin_b):
    return y * cos + pltpu.roll(y, 96, 1) * sin_a + pltpu.roll(y, 32, 1) * sin_b


def _dot(a, b):
    return jnp.dot(a, b, preferred_element_type=jnp.float32)


def _inproj_kernel(x_ref, nw_ref, w_ref, gqa_ref, gka_ref, gqb_ref, gkb_ref,
                   tab_ref,
                   qat_ref, ka_ref, vat_ref, za_ref, qbt_ref, kb_ref, vbt_ref, zb_ref,
                   xn_ref):
    x = x_ref[...]
    ms = jnp.mean(x * x, axis=-1, keepdims=True)
    xn_ref[...] = (x * lax.rsqrt(ms + EPS) * nw_ref[...]).astype(jnp.bfloat16)

    def column_tile(j):
        return _dot(xn_ref[...], w_ref[:, j * TN:(j + 1) * TN])

    def normed_roped(a, gain_ref):
        cos, sin_a, sin_b = (tab_ref[:, t * HEAD_DIM:(t + 1) * HEAD_DIM] for t in range(3))
        return _rope(_head_norm(a, gain_ref[...]), cos, sin_a, sin_b)

    def store_heads(ref, first, vals):
        for h, v in enumerate(vals):
            ref[:, (first + h) * HEAD_DIM:(first + h + 1) * HEAD_DIM] = v.astype(ref.dtype)

    heads_per_tile = TN // HEAD_DIM

    def store_values_t(ref, first, vals):
        ones_row = (lax.broadcasted_iota(jnp.int32, (VT_EXTRA_ROWS, vals[0].shape[0]), 0) == 0)
        for h, a in enumerate(vals):
            ref[first + h, 0, :HEAD_DIM, :] = a.T.astype(ref.dtype)
            ref[first + h, 0, HEAD_DIM:, :] = ones_row.astype(ref.dtype)

    def epilogue(j, acc):
        heads = [acc[:, h * HEAD_DIM:(h + 1) * HEAD_DIM] for h in range(heads_per_tile)]
        if j < 2:
            for h, a in enumerate(heads):
                q = normed_roped(a, gqa_ref) * Q_SCALE
                qat_ref[j * heads_per_tile + h] = q.T.astype(qat_ref.dtype)
        elif j == 2:
            store_heads(ka_ref, 0, [normed_roped(a, gka_ref) for a in heads[:N_KV_A]])
            store_values_t(vat_ref, 0, heads[N_KV_A:])
        elif j < 5:
            za_ref[:, (j - 3) * TN:(j - 2) * TN] = _silu(acc)
        elif j < 7:
            for h, a in enumerate(heads):
                q = _head_norm(a, gqb_ref[...]) * Q_SCALE
                qbt_ref[(j - 5) * heads_per_tile + h] = q.T.astype(qbt_ref.dtype)
        elif j < 9:
            store_heads(kb_ref, (j - 7) * heads_per_tile,
                        [_head_norm(a, gkb_ref[...]) for a in heads])
        elif j < 11:
            store_values_t(vbt_ref, (j - 9) * heads_per_tile, heads)
        else:
            zb_ref[:, (j - 11) * TN:(j - 10) * TN] = _silu(acc)

    acc = column_tile(0)
    for j in range(N_COL_TILES):
        nxt = column_tile(j + 1) if j + 1 < N_COL_TILES else None
        epilogue(j, acc)
        acc = nxt


def _in_projection(x2d, seq, tm, chunk, norm_w, w_in_bf16, gains, tabs):
    m = x2d.shape[0]
    n_tab = seq // tm
    sub = chunk // tm
    row = lambda i: (i, 0)
    const = lambda i: (0, 0)
    bf, f32 = jnp.bfloat16, jnp.float32
    out_shape = [
        jax.ShapeDtypeStruct((N_HEADS_A, HEAD_DIM, m), bf), jax.ShapeDtypeStruct((m, D_KV_A), bf),
        jax.ShapeDtypeStruct((N_KV_A, m // chunk, VT_ROWS, chunk), bf),
        jax.ShapeDtypeStruct((m, D_A), f32),
        jax.ShapeDtypeStruct((N_HEADS_B, HEAD_DIM, m), bf), jax.ShapeDtypeStruct((m, D_B), bf),
        jax.ShapeDtypeStruct((N_HEADS_B, m // tm, VT_ROWS, tm), bf),
        jax.ShapeDtypeStruct((m, D_B), f32),
    ]
    out_specs = [
        pl.BlockSpec((N_HEADS_A, HEAD_DIM, tm), lambda i: (0, 0, i)),
        pl.BlockSpec((tm, D_KV_A), row),
        pl.BlockSpec((N_KV_A, 1, VT_ROWS, tm), lambda i: (0, i // sub, 0, i % sub)),
        pl.BlockSpec((tm, D_A), row),
        pl.BlockSpec((N_HEADS_B, HEAD_DIM, tm), lambda i: (0, 0, i)),
        pl.BlockSpec((tm, D_B), row),
        pl.BlockSpec((N_HEADS_B, 1, VT_ROWS, tm), lambda i: (0, i, 0, 0)),
        pl.BlockSpec((tm, D_B), row),
    ]
    gain_spec = pl.BlockSpec((1, HEAD_DIM), const)
    tab_spec = pl.BlockSpec((tm, 3 * HEAD_DIM), lambda i: (i % n_tab, 0))
    return pl.pallas_call(
        _inproj_kernel,
        grid=(m // tm,),
        in_specs=[
            pl.BlockSpec((tm, D_MODEL), row),
            pl.BlockSpec((1, D_MODEL), const),
            pl.BlockSpec((D_MODEL, D_IN), const, pipeline_mode=pl.Buffered(1)),
            gain_spec, gain_spec, gain_spec, gain_spec,
            tab_spec,
        ],
        out_specs=out_specs,
        out_shape=out_shape,
        scratch_shapes=[pltpu.VMEM((tm, D_MODEL), bf)],
        name="in_projection",
        compiler_params=pltpu.CompilerParams(
            dimension_semantics=("arbitrary",),
            vmem_limit_bytes=VMEM_LIMIT),
    )(x2d, norm_w, w_in_bf16, *gains, tabs)


def _gqa_kernel(safe_ref, qt_ref, k_ref, vt_ref, km_ref, vmt_ref, g_ref, o_ref, acc_ref, s_ref):
    tq = qt_ref.shape[2]
    n_chunks, chunk = vt_ref.shape[1], vt_ref.shape[3]
    width = acc_ref.shape[2]
    n_slabs = acc_ref.shape[0]
    heads_per_slab = width // tq
    km = km_ref[:N_META, :]
    vmt = vmt_ref[0, 0]
    meta_pad = jnp.zeros((META_PAD - N_META, width), jnp.bfloat16)
    qts = [jnp.concatenate([qt_ref[i * heads_per_slab + j] for j in range(heads_per_slab)], axis=1)
           if heads_per_slab > 1 else qt_ref[i] for i in range(n_slabs)]

    def meta_values(p):
        return _dot(vmt, jnp.concatenate([p.astype(jnp.bfloat16), meta_pad], axis=0))

    def keys(c):
        if isinstance(c, int):
            return k_ref[c * chunk:(c + 1) * chunk, :]
        return k_ref[pl.ds(pl.multiple_of(c * chunk, chunk), chunk), :]

    def attend(init, consume):
        def chunk_stages(c, state, has_next):
            state = list(state)
            k = keys(c)
            vt = vt_ref[0, c]
            for i in range(n_slabs):
                if i + 1 < n_slabs:
                    s_ref[(i + 1) % 2] = _dot(k, qts[i + 1])
                elif has_next:
                    s_ref[0] = _dot(keys(c + 1), qts[0])
                state[i] = consume(i, s_ref[i % 2], vt, state[i])
            return tuple(state)

        s_ref[0] = _dot(keys(0), qts[0])
        s_meta = [_dot(km, qt) for qt in qts]
        state = tuple(init(i, s) for i, s in enumerate(s_meta))
        state = chunk_stages(0, state, n_chunks > 1)
        if n_chunks > 1:
            state = lax.fori_loop(1, n_chunks - 1, lambda c, st: chunk_stages(c, st, True), state)
            chunk_stages(n_chunks - 1, state, False)
        for i in range(n_slabs):
            acc = acc_ref[i]
            o_slab = acc[:HEAD_DIM] * (1.0 / acc[HEAD_DIM:HEAD_DIM + 1])
            for j in range(heads_per_slab):
                h = i * heads_per_slab + j
                sl = slice(h * HEAD_DIM, (h + 1) * HEAD_DIM)
                o = o_slab[:, j * tq:(j + 1) * tq].T
                o_ref[:, sl] = (o * g_ref[:, sl]).astype(o_ref.dtype)

    safe = safe_ref[0] != 0

    @pl.when(safe)
    def _():
        def init(i, s):
            acc_ref[i] = meta_values(jnp.exp2(s))
            return 0

        def consume(i, s, vt, state):
            acc_ref[i] += _dot(vt, jnp.exp2(s).astype(jnp.bfloat16))
            return state

        attend(init, consume)

    @pl.when(jnp.logical_not(safe))
    def _():
        def init(i, s):
            m = jnp.max(s, axis=0, keepdims=True)
            acc_ref[i] = meta_values(jnp.exp2(s - m))
            return m

        def consume(i, s, vt, m):
            m_new = jnp.maximum(m, jnp.max(s, axis=0, keepdims=True))
            p = jnp.exp2(s - m_new)
            acc_ref[i] = jnp.exp2(m - m_new) * acc_ref[i] + _dot(vt, p.astype(jnp.bfloat16))
            return m_new

        attend(init, consume)


def _global_attention(safe, qat, ka, vat, gate, km, vmt, batch, seq, tq):
    m = gate.shape[0]
    nq = seq // tq
    chunk = vat.shape[3]
    n_chunks = seq // chunk
    qmap = lambda b, g, i: (b * nq + i, g)
    width = GROUP_A * HEAD_DIM
    return pl.pallas_call(
        _gqa_kernel,
        grid=(batch, N_KV_A, nq),
        in_specs=[
            pl.BlockSpec(memory_space=pltpu.SMEM),
            pl.BlockSpec((GROUP_A, HEAD_DIM, tq), lambda b, g, i: (g, 0, b * nq + i)),
            pl.BlockSpec((seq, HEAD_DIM), lambda b, g, i: (b, g)),
            pl.BlockSpec((1, n_chunks, VT_ROWS, chunk), lambda b, g, i: (g, b, 0, 0)),
            pl.BlockSpec((META_PAD, HEAD_DIM), lambda b, g, i: (0, g)),
            pl.BlockSpec((1, 1, VT_ROWS, META_PAD), lambda b, g, i: (g, 0, 0, 0)),
            pl.BlockSpec((tq, width), qmap),
        ],
        out_specs=pl.BlockSpec((tq, width), qmap),
        out_shape=jax.ShapeDtypeStruct((m, D_A), jnp.bfloat16),
        scratch_shapes=[pltpu.VMEM((GROUP_A * tq // GQA_SLAB_WIDTH, VT_ROWS, GQA_SLAB_WIDTH), jnp.float32),
                        pltpu.VMEM((2, chunk, GQA_SLAB_WIDTH), jnp.float32)],
        name="global_attention",
        compiler_params=pltpu.CompilerParams(
            dimension_semantics=("arbitrary", "arbitrary", "arbitrary"),
            vmem_limit_bytes=VMEM_LIMIT),
    )(safe, qat, ka, vat, km, vmt, gate)


def _nbr_kernel(safe_ref, qt_ref, k_ref, vt_ref, km_ref, vmt_ref, bias_ref, g_ref, o_ref, *, n_tiles):
    step = pl.program_id(2)
    bf = jnp.bfloat16
    km = km_ref[:N_META, :]
    vmt = vmt_ref[0, 0]
    meta_pad = jnp.zeros((META_PAD - N_META, NB_Q), bf)
    chunks_per_window = NB_K // NB_VCHUNK

    def run(probs_fn):
        tiles = []
        tiles_per_step = qt_ref.shape[2] // NB_Q
        for u in range(tiles_per_step):
            t = step * tiles_per_step + u
            cls = jnp.where(t == 0, 0, jnp.where(t == n_tiles - 1, 2, 1))
            first_chunk = jnp.clip(t - 1, 0, n_tiles - chunks_per_window)
            tiles.append((u, cls, first_chunk))
        scores = []
        for u, cls, first_chunk in tiles:
            qt = qt_ref[0, :, u * NB_Q:(u + 1) * NB_Q]
            kstart = pl.multiple_of(first_chunk * NB_VCHUNK, NB_VCHUNK)
            s_win = _dot(k_ref[pl.ds(kstart, NB_K), :], qt) + bias_ref[0, cls]
            scores.append((s_win, _dot(km, qt)))
        probs = [probs_fn(s_win, s_meta) for s_win, s_meta in scores]
        for (u, cls, first_chunk), (p_win, p_meta) in zip(tiles, probs):
            acc = _dot(vmt, jnp.concatenate([p_meta, meta_pad], axis=0))
            for j in range(chunks_per_window):
                acc += _dot(vt_ref[0, first_chunk + j], p_win[j * NB_VCHUNK:(j + 1) * NB_VCHUNK])
            o = (acc[:HEAD_DIM] * (1.0 / acc[HEAD_DIM:HEAD_DIM + 1])).T
            rows = slice(u * NB_Q, (u + 1) * NB_Q)
            o_ref[rows, :] = (o * g_ref[rows, :]).astype(o_ref.dtype)

    def probs_safe(s_win, s_meta):
        return jnp.exp2(s_win).astype(bf), jnp.exp2(s_meta).astype(bf)

    def probs_max(s_win, s_meta):
        m = jnp.maximum(jnp.max(s_win, axis=0, keepdims=True), jnp.max(s_meta, axis=0, keepdims=True))
        return jnp.exp2(s_win - m).astype(bf), jnp.exp2(s_meta - m).astype(bf)

    safe = safe_ref[0] != 0

    @pl.when(safe)
    def _():
        run(probs_safe)

    @pl.when(jnp.logical_not(safe))
    def _():
        run(probs_max)


def _neighbourhood_attention(safe, qbt, kb, vbt, gate, km, vmt, bias, batch, seq):
    m = gate.shape[0]
    tqb = min(NB_TILES_PER_STEP * NB_Q, seq)
    nq = seq // tqb
    n_vchunks = seq // NB_VCHUNK
    assert vbt.shape[3] == NB_VCHUNK
    qmap = lambda h, b, i: (b * nq + i, h)
    return pl.pallas_call(
        functools.partial(_nbr_kernel, n_tiles=seq // NB_Q),
        grid=(N_HEADS_B, batch, nq),
        in_specs=[
            pl.BlockSpec(memory_space=pltpu.SMEM),
            pl.BlockSpec((1, HEAD_DIM, tqb), lambda h, b, i: (h, 0, b * nq + i)),
            pl.BlockSpec((seq, HEAD_DIM), lambda h, b, i: (b, h)),
            pl.BlockSpec((1, n_vchunks, VT_ROWS, NB_VCHUNK), lambda h, b, i: (h, b, 0, 0)),
            pl.BlockSpec((META_PAD, HEAD_DIM), lambda h, b, i: (0, h)),
            pl.BlockSpec((1, 1, VT_ROWS, META_PAD), lambda h, b, i: (h, 0, 0, 0)),
            pl.BlockSpec((1, NB_CLASSES, NB_K, NB_Q), lambda h, b, i: (h, 0, 0, 0)),
            pl.BlockSpec((tqb, HEAD_DIM), qmap),
        ],
        out_specs=pl.BlockSpec((tqb, HEAD_DIM), qmap),
        out_shape=jax.ShapeDtypeStruct((m, D_B), jnp.bfloat16),
        name="neighbourhood_attention",
        compiler_params=pltpu.CompilerParams(
            dimension_semantics=("arbitrary", "arbitrary", "arbitrary"),
            vmem_limit_bytes=VMEM_LIMIT),
    )(safe, qbt, kb, vbt, km, vmt, bias, gate)


def _outproj_kernel(x_ref, ya_ref, yb_ref, w_ref, o_ref):
    y = _dot(ya_ref[...], w_ref[:D_A, :]) + _dot(yb_ref[...], w_ref[D_A:, :])
    o_ref[...] = x_ref[...] + y


def _out_projection(x2d, ya, yb, w_out_bf16, tm):
    m = x2d.shape[0]
    row = lambda i: (i, 0)
    return pl.pallas_call(
        _outproj_kernel,
        grid=(m // tm,),
        in_specs=[
            pl.BlockSpec((tm, D_MODEL), row),
            pl.BlockSpec((tm, D_A), row),
            pl.BlockSpec((tm, D_B), row),
            pl.BlockSpec((D_A + D_B, D_MODEL), lambda i: (0, 0)),
        ],
        out_specs=pl.BlockSpec((tm, D_MODEL), row),
        out_shape=jax.ShapeDtypeStruct((m, D_MODEL), jnp.float32),
        name="out_projection",
        compiler_params=pltpu.CompilerParams(
            dimension_semantics=("arbitrary",),
            vmem_limit_bytes=VMEM_LIMIT),
    )(x2d, ya, yb, w_out_bf16)


def _rope_tables(rows, cols):
    half = HEAD_DIM // 2
    inv_freq = ROPE_THETA ** (-jnp.arange(0, half, 2, dtype=jnp.float32) / half)
    ang_r = rows.astype(jnp.float32)[:, None] * inv_freq[None, :]
    ang_c = cols.astype(jnp.float32)[:, None] * inv_freq[None, :]
    n_r, n_c = rows.shape[0], cols.shape[0]

    def grid(r_part, c_part):
        r_part = jnp.broadcast_to(r_part[:, None, :], (n_r, n_c, r_part.shape[-1]))
        c_part = jnp.broadcast_to(c_part[None, :, :], (n_r, n_c, c_part.shape[-1]))
        return jnp.concatenate([r_part, c_part], axis=-1).reshape(n_r * n_c, HEAD_DIM)

    cr, sr, cc, sc = jnp.cos(ang_r), jnp.sin(ang_r), jnp.cos(ang_c), jnp.sin(ang_c)
    zr, zc = jnp.zeros_like(sr), jnp.zeros_like(sc)
    cos = grid(jnp.concatenate([cr, cr], -1), jnp.concatenate([cc, cc], -1))
    sin_a = grid(jnp.concatenate([-sr, zr], -1), jnp.concatenate([-sc, zc], -1))
    sin_b = grid(jnp.concatenate([zr, sr], -1), jnp.concatenate([zc, sc], -1))
    return jnp.concatenate([cos, sin_a, sin_b], axis=-1)


def _nbr_bias_tables(rpb):
    rows = 32
    tile_r = np.array([0, 2 * NB_QROWS, rows - NB_QROWS])[:, None, None]
    dr = np.arange(NB_QROWS)[None, :, None]
    jj = np.arange(NB_KROWS)[None, None, :]
    k0 = np.clip(tile_r - NB_QROWS, 0, rows - NB_KROWS)
    r = tile_r + dr
    kr = k0 + jj
    r0 = np.clip(r - WIN_R // 2, 0, rows - WIN_R)
    row_valid = (kr >= r0) & (kr < r0 + WIN_R)
    off_r = np.clip(kr - r + (WIN_R - 1), 0, 2 * WIN_R - 2)
    n_off_c = 2 * WIN_C - 1
    pos = np.arange(GRID_W)
    onehot = (pos[None, :, None] - pos[None, None, :] + (WIN_C - 1)
              == np.arange(n_off_c)[:, None, None]).astype(np.float32)
    toeplitz = jnp.einsum('hro,okc->hrkc', rpb.astype(jnp.float32) * LOG2E, onehot,
                          precision=lax.Precision.HIGHEST)

    def assemble(toep_ref, out_ref):
        kc = lax.broadcasted_iota(jnp.int32, (GRID_W, GRID_W), 0)
        c = lax.broadcasted_iota(jnp.int32, (GRID_W, GRID_W), 1)
        c0 = jnp.clip(c - WIN_C // 2, 0, GRID_W - WIN_C)
        col_valid = (kc >= c0) & (kc < c0 + WIN_C)
        masked = jnp.full((GRID_W, GRID_W), MASK_VALUE, jnp.float32)
        for cls in range(NB_CLASSES):
            for j in range(NB_KROWS):
                blocks = [jnp.where(col_valid, toep_ref[0, int(off_r[cls, d, j])], MASK_VALUE)
                          if row_valid[cls, d, j] else masked for d in range(NB_QROWS)]
                out_ref[0, cls, j * GRID_W:(j + 1) * GRID_W, :] = jnp.concatenate(blocks, axis=1)

    return pl.pallas_call(
        assemble,
        grid=(N_HEADS_B,),
        in_specs=[pl.BlockSpec((1, 2 * WIN_R - 1, GRID_W, GRID_W), lambda h: (h, 0, 0, 0))],
        out_specs=pl.BlockSpec((1, NB_CLASSES, NB_K, NB_Q), lambda h: (h, 0, 0, 0)),
        out_shape=jax.ShapeDtypeStruct((N_HEADS_B, NB_CLASSES, NB_K, NB_Q), jnp.float32),
        name="nbr_bias_table",
    )(toeplitz)


def kernel(x_prompt, x_sample, meta_tokens, norm_w, w_in, q_norm_a, k_norm_a, q_norm_b, k_norm_b, rpb, w_out):
    w_in_bf = w_in[0].astype(jnp.bfloat16)
    w_out_bf = w_out[0].astype(jnp.bfloat16)
    nw = norm_w[0].reshape(1, D_MODEL)
    gains = [g[0].reshape(1, HEAD_DIM) for g in (q_norm_a, k_norm_a, q_norm_b, k_norm_b)]
    bias = _nbr_bias_tables(rpb[0])

    gain_max = jnp.max(jnp.abs(jnp.concatenate(gains, axis=0)), axis=1)
    score_bound_a = HEAD_DIM * Q_SCALE * gain_max[0] * gain_max[1]
    safe_a = (score_bound_a <= SAFE_SCORE_BOUND).astype(jnp.int32).reshape(1)
    score_bound_b = (HEAD_DIM * Q_SCALE * gain_max[2] * gain_max[3]
                     + LOG2E * jnp.max(jnp.abs(rpb[0])))
    safe_b = (score_bound_b <= SAFE_SCORE_BOUND).astype(jnp.int32).reshape(1)

    meta_tabs = _rope_tables(jnp.full((1,), -1, jnp.int32), jnp.arange(META_PAD, dtype=jnp.int32))
    meta_x = jnp.pad(meta_tokens, ((0, META_PAD - N_META), (0, 0)))
    meta_out = _in_projection(meta_x, META_PAD, META_PAD, META_PAD, nw, w_in_bf, gains, meta_tabs)
    _, km_a, vmt_a, _, _, km_b, vmt_b, _ = meta_out

    max_seq = max(x_prompt.shape[1], x_sample.shape[1])
    tabs = _rope_tables(jnp.arange(max_seq // GRID_W, dtype=jnp.int32),
                        jnp.arange(GRID_W, dtype=jnp.int32))

    def encode(x):
        batch, seq, _ = x.shape
        x2d = x.reshape(batch * seq, D_MODEL)
        qat, ka, vat, ga, qbt, kb, vbt, gb = _in_projection(
            x2d, seq, IN_ROW_TILE, GQA_KEY_CHUNK, nw, w_in_bf, gains, tabs)
        ya = _global_attention(safe_a, qat, ka, vat, ga, km_a, vmt_a, batch, seq, GQA_Q_TILE)
        yb = _neighbourhood_attention(safe_b, qbt, kb, vbt, gb, km_b, vmt_b, bias, batch, seq)
        y = _out_projection(x2d, ya, yb, w_out_bf, OUT_ROW_TILE)
        return y.reshape(batch, seq, D_MODEL)

    return (encode(x_prompt), encode(x_sample))
```
